```python
import math
import jax, jax.numpy as jnp
from jax import lax
import numpy as np

D_MODEL = 1024
BATCH = 8
SEQ = 2048
DEPTH = 4
DEC_BATCH = 128
DEC_SEQ = 4
PAST_LEN = 16384
PAGE_SIZE = 128

N_MIXERS = 2
N_CONV_LAYERS = (DEPTH + 1) // 2
N_GLA_LAYERS = DEPTH // 2
D_FF = ((8 * D_MODEL // 3 + 255) // 256) * 256
CONV_WIDTH = 31
GLA_HEADS = 4
GLA_DK = D_MODEL // 2 // GLA_HEADS
GLA_DV = D_MODEL // GLA_HEADS
GLA_DKT = GLA_HEADS * GLA_DK
GLA_DVT = GLA_HEADS * GLA_DV
GLA_GATE_RANK = 16
GLA_TAU = 16.0
GLA_CHUNK = 32
EPS = 1e-6

kernel_name = 'macaron_conformer_conv_gla_hybrid_step'


def rms_norm(x, g):
    xf = x.astype(jnp.float32)
    y = xf * lax.rsqrt(jnp.mean(xf * xf, axis=-1, keepdims=True) + EPS)
    return (y * g.astype(jnp.float32)).astype(x.dtype)


def layer_norm(x, g, b):
    xf = x.astype(jnp.float32)
    mu = jnp.mean(xf, axis=-1, keepdims=True)
    var = jnp.mean(jnp.square(xf - mu), axis=-1, keepdims=True)
    y = (xf - mu) * lax.rsqrt(var + EPS) * g.astype(jnp.float32) + b.astype(jnp.float32)
    return y.astype(x.dtype)


def swiglu_ffn(h, w_gate, w_up, w_down):
    return (jax.nn.silu(h @ w_gate) * (h @ w_up)) @ w_down


def conv_module(h, state, w_pw1, b_pw1, w_dw, b_dw, ln_g, ln_b, w_pw2, b_pw2):
    u = h @ w_pw1 + b_pw1
    a, gate = jnp.split(u, 2, axis=-1)
    u = a * jax.nn.sigmoid(gate)
    full = jnp.concatenate([state.astype(u.dtype), u], axis=1)
    c = lax.conv_general_dilated(full, w_dw[:, None, :].astype(u.dtype), window_strides=(1,),
                                 padding='VALID', dimension_numbers=('NWC', 'WIO', 'NWC'),
                                 feature_group_count=D_MODEL) + b_dw
    c = jax.nn.silu(layer_norm(c, ln_g, ln_b))
    y = c @ w_pw2 + b_pw2
    new_state = full[:, -(CONV_WIDTH - 1):]
    return y, new_state


def gla_chunked(q, k, v, log_a, s0):
    B, L, H, DK = q.shape
    C = math.gcd(L, GLA_CHUNK)
    N = L // C
    r = lambda t: t.reshape(B, N, C, H, t.shape[-1])
    q, k, v, log_a = r(q), r(k), r(v), r(log_a)
    b = jnp.cumsum(log_a, axis=2)
    b_last = b[:, :, -1:]
    q_dec = q * jnp.exp(b)
    k_inv = k * jnp.exp(-b)
    k_end = k * jnp.exp(b_last - b)
    scores = jnp.einsum('bnchk,bnshk->bnhcs', q_dec, k_inv)
    mask = jnp.tril(jnp.ones((C, C), dtype=bool))
    scores = jnp.where(mask, scores, 0.0)
    o_intra = jnp.einsum('bnhcs,bnshv->bnchv', scores, v)

    def step(S, xs):
        qd, ke, vc, dl = xs
        o = jnp.einsum('bchk,bhkv->bchv', qd, S)
        S = S * jnp.exp(dl)[..., None] + jnp.einsum('bchk,bchv->bhkv', ke, vc)
        return S, o

    xs = (q_dec.swapaxes(0, 1), k_end.swapaxes(0, 1), v.swapaxes(0, 1), b_last[:, :, 0].swapaxes(0, 1))
    S, o_inter = lax.scan(step, s0, xs)
    o = o_intra + o_inter.swapaxes(0, 1)
    return o.reshape(B, L, H, v.shape[-1]), S


def gla_mixer(h, state, w_in, w_gate2, b_gate, g_onorm, w_out):
    B, L, _ = h.shape
    p = h @ w_in
    q, k, v, r, z = jnp.split(p, [GLA_DKT, 2 * GLA_DKT, 2 * GLA_DKT + GLA_DVT, 2 * GLA_DKT + 2 * GLA_DVT], axis=-1)
    log_a = jax.nn.log_sigmoid((z @ w_gate2 + b_gate).astype(jnp.float32)) / GLA_TAU
    hd = lambda t, d: t.astype(jnp.float32).reshape(B, L, GLA_HEADS, d)
    o, S = gla_chunked(hd(q, GLA_DK) * (GLA_DK ** -0.5), hd(k, GLA_DK), hd(v, GLA_DV),
                       log_a.reshape(B, L, GLA_HEADS, GLA_DK), state.astype(jnp.float32))
    o = o * lax.rsqrt(jnp.mean(o * o, axis=-1, keepdims=True) + EPS) * g_onorm.astype(jnp.float32)
    o = o.reshape(B, L, GLA_DVT).astype(h.dtype) * jax.nn.silu(r)
    return o @ w_out, S.astype(state.dtype)


def trunk(x, conv_state, gla_state, w):
    (norm_ffn_pre, norm_mix, norm_ffn_post, norm_final, ffn_w_gate, ffn_w_up, ffn_w_down,
     conv_w_pw1, conv_b_pw1, conv_w_dw, conv_b_dw, conv_ln_g, conv_ln_b, conv_w_pw2, conv_b_pw2,
     gla_w_in, gla_w_gate2, gla_b_gate, gla_onorm_g, gla_w_out) = w
    new_conv, new_gla = [], []
    for i in range(DEPTH):
        x = x + 0.5 * swiglu_ffn(rms_norm(x, norm_ffn_pre[i]), ffn_w_gate[i, 0], ffn_w_up[i, 0], ffn_w_down[i, 0])
        h = rms_norm(x, norm_mix[i])
        j = i // N_MIXERS
        if i % N_MIXERS == 0:
            y, s = conv_module(h, conv_state[j], conv_w_pw1[j], conv_b_pw1[j], conv_w_dw[j], conv_b_dw[j],
                               conv_ln_g[j], conv_ln_b[j], conv_w_pw2[j], conv_b_pw2[j])
            new_conv.append(s)
        else:
            y, s = gla_mixer(h, gla_state[j], gla_w_in[j], gla_w_gate2[j], gla_b_gate[j], gla_onorm_g[j], gla_w_out[j])
            new_gla.append(s)
        x = x + y
        x = x + 0.5 * swiglu_ffn(rms_norm(x, norm_ffn_post[i]), ffn_w_gate[i, 1], ffn_w_up[i, 1], ffn_w_down[i, 1])
    return rms_norm(x, norm_final), jnp.stack(new_conv), jnp.stack(new_gla)


def setup_inputs(seed: int = 0) -> dict:
    key = jax.random.key(seed)
    ks = iter(jax.random.split(key, 32))
    nrm = lambda shape, scale: jax.random.normal(next(ks), shape, jnp.float32) * scale
    gain = lambda shape: 1.0 + 0.02 * jax.random.normal(next(ks), shape, jnp.float32)
    d_in = 2 * GLA_DKT + 2 * GLA_DVT + GLA_GATE_RANK
    return {
        'x_prompt': nrm((BATCH, SEQ, D_MODEL), 1.0),
        'x_sample': nrm((DEC_BATCH, DEC_SEQ, D_MODEL), 1.0),
        'state_conv': nrm((N_CONV_LAYERS, DEC_BATCH, CONV_WIDTH - 1, D_MODEL), 0.5),
        'state_gla': nrm((N_GLA_LAYERS, DEC_BATCH, GLA_HEADS, GLA_DK, GLA_DV), 1.0),
        'norm_ffn_pre': gain((DEPTH, D_MODEL)),
        'norm_mix': gain((DEPTH, D_MODEL)),
        'norm_ffn_post': gain((DEPTH, D_MODEL)),
        'norm_final': gain((D_MODEL,)),
        'ffn_w_gate': nrm((DEPTH, 2, D_MODEL, D_FF), D_MODEL ** -0.5),
        'ffn_w_up': nrm((DEPTH, 2, D_MODEL, D_FF), D_MODEL ** -0.5),
        'ffn_w_down': nrm((DEPTH, 2, D_FF, D_MODEL), D_FF ** -0.5),
        'conv_w_pw1': nrm((N_CONV_LAYERS, D_MODEL, 2 * D_MODEL), D_MODEL ** -0.5),
        'conv_b_pw1': nrm((N_CONV_LAYERS, 2 * D_MODEL), 0.02),
        'conv_w_dw': nrm((N_CONV_LAYERS, CONV_WIDTH, D_MODEL), CONV_WIDTH ** -0.5),
        'conv_b_dw': nrm((N_CONV_LAYERS, D_MODEL), 0.02),
        'conv_ln_g': gain((N_CONV_LAYERS, D_MODEL)),
        'conv_ln_b': nrm((N_CONV_LAYERS, D_MODEL), 0.02),
        'conv_w_pw2': nrm((N_CONV_LAYERS, D_MODEL, D_MODEL), D_MODEL ** -0.5),
        'conv_b_pw2': nrm((N_CONV_LAYERS, D_MODEL), 0.02),
        'gla_w_in': nrm((N_GLA_LAYERS, D_MODEL, d_in), D_MODEL ** -0.5),
        'gla_w_gate2': nrm((N_GLA_LAYERS, GLA_GATE_RANK, GLA_DKT), GLA_GATE_RANK ** -0.5),
        'gla_b_gate': nrm((N_GLA_LAYERS, GLA_DKT), 0.1),
        'gla_onorm_g': gain((N_GLA_LAYERS, GLA_DV)),
        'gla_w_out': nrm((N_GLA_LAYERS, GLA_DVT, D_MODEL), GLA_DVT ** -0.5),
    }


def reference(x_prompt, x_sample, state_conv, state_gla,
              norm_ffn_pre, norm_mix, norm_ffn_post, norm_final, ffn_w_gate, ffn_w_up, ffn_w_down,
              conv_w_pw1, conv_b_pw1, conv_w_dw, conv_b_dw, conv_ln_g, conv_ln_b, conv_w_pw2, conv_b_pw2,
              gla_w_in, gla_w_gate2, gla_b_gate, gla_onorm_g, gla_w_out):
    w = (norm_ffn_pre, norm_mix, norm_ffn_post, norm_final, ffn_w_gate, ffn_w_up, ffn_w_down,
         conv_w_pw1, conv_b_pw1, conv_w_dw, conv_b_dw, conv_ln_g, conv_ln_b, conv_w_pw2, conv_b_pw2,
         gla_w_in, gla_w_gate2, gla_b_gate, gla_onorm_g, gla_w_out)
    b = x_prompt.shape[0]
    conv0 = jnp.zeros((N_CONV_LAYERS, b, CONV_WIDTH - 1, D_MODEL), x_prompt.dtype)
    gla0 = jnp.zeros((N_GLA_LAYERS, b, GLA_HEADS, GLA_DK, GLA_DV), x_prompt.dtype)
    y_prompt, new_conv_prompt, new_gla_prompt = trunk(x_prompt, conv0, gla0, w)
    y_sample, new_conv_sample, new_gla_sample = trunk(x_sample, state_conv, state_gla, w)
    return (y_prompt, y_sample, new_conv_prompt, new_gla_prompt, new_conv_sample, new_gla_sample)
```

```python
import functools

import jax
import jax.numpy as jnp
from jax import lax
from jax.experimental import pallas as pl
from jax.experimental.pallas import tpu as pltpu

f32 = jnp.float32
bf16 = jnp.bfloat16

D_MODEL = 1024
D_FF = 2816
DEPTH = 4
CONV_WIDTH = 31
CONV_STATE = CONV_WIDTH - 1
GLA_HEADS = 4
GLA_DK = 128
GLA_DV = 256
GLA_DKT = GLA_HEADS * GLA_DK
GLA_DVT = GLA_HEADS * GLA_DV
GLA_GATE_RANK = 16
GLA_TAU = 16.0
EPS = 1e-6

LANES = 128
SUBLANES = 8
N_SLABS = D_MODEL // LANES
VMEM_LIMIT = 56 * 1024 * 1024

ROW_TILE = 512
FFN_CHUNK = 256
CONV_TILE = 512
CONV_HALO = 32
CONV_STRIDE = 4
CONV_BLOCK = SUBLANES * CONV_STRIDE
CONV_SAMPLE_BATCH = 16
GLA_CHUNK = 256
GLA_SUB = 32
GLA_SAMPLE_BATCH = 8


def _params(*semantics):
    return pltpu.CompilerParams(dimension_semantics=semantics, vmem_limit_bytes=VMEM_LIMIT)


def _resident(shape):
    zeros = (0,) * len(shape)
    return pl.BlockSpec(shape, lambda *_: zeros, pipeline_mode=pl.Buffered(1))


def _rows(tile, width):
    return pl.BlockSpec((tile, width), lambda i: (i, 0))


def _rms(x, g):
    return x * lax.rsqrt(jnp.mean(x * x, axis=-1, keepdims=True) + EPS) * g


def _silu(x):
    return x * jax.nn.sigmoid(x)


def _dot(a, b):
    return jnp.dot(a, b, preferred_element_type=f32)


def _dot_t_rhs(a, b):
    return lax.dot_general(a, b, (((1,), (1,)), ((), ())), preferred_element_type=f32)


def _dot_t_lhs(a, b):
    return lax.dot_general(a, b, (((0,), (0,)), ((), ())), preferred_element_type=f32)


def _ffn_body(x_ref, g_ref, wg_ref, wu_ref, wd_ref, gf_ref, o_ref, a_scr, *, final_norm):
    x = x_ref[...]
    h = _rms(x, g_ref[...]).astype(bf16)
    for j in range(D_FF // FFN_CHUNK):
        cols = pl.ds(j * FFN_CHUNK, FFN_CHUNK)
        gate = _dot(h, wg_ref[:, cols])
        up = _dot(h, wu_ref[:, cols])
        a_scr[:, cols] = (_silu(gate) * up).astype(bf16)
    y = x + 0.5 * _dot(a_scr[...], wd_ref[...])
    if final_norm:
        y = _rms(y, gf_ref[...])
    o_ref[...] = y


def _ffn(x, g, wg, wu, wd, gf, *, final_norm):
    rows = x.shape[0]
    tile = min(ROW_TILE, rows)
    return pl.pallas_call(
        functools.partial(_ffn_body, final_norm=final_norm),
        grid=(rows // tile,),
        in_specs=[_rows(tile, D_MODEL), _resident((1, D_MODEL)), _resident((D_MODEL, D_FF)),
                  _resident((D_MODEL, D_FF)), _resident((D_FF, D_MODEL)), _resident((1, D_MODEL))],
        out_specs=_rows(tile, D_MODEL),
        out_shape=jax.ShapeDtypeStruct((rows, D_MODEL), f32),
        scratch_shapes=[pltpu.VMEM((tile, D_FF), bf16)],
        compiler_params=_params("arbitrary"),
        name="ffn",
    )(x, g, wg, wu, wd, gf)


def _conv_a_body(x_ref, g_ref, wa_ref, wb_ref, ba_ref, bb_ref, u_ref):
    h = _rms(x_ref[...], g_ref[...]).astype(bf16)
    a = _dot(h, wa_ref[...]) + ba_ref[...]
    gate = _dot(h, wb_ref[...]) + bb_ref[...]
    u_ref[...] = a * jax.nn.sigmoid(gate)


def _conv_a(x, g, wa, wb, ba, bb):
    rows = x.shape[0]
    tile = min(ROW_TILE, rows)
    return pl.pallas_call(
        _conv_a_body,
        grid=(rows // tile,),
        in_specs=[_rows(tile, D_MODEL), _resident((1, D_MODEL)), _resident((D_MODEL, D_MODEL)),
                  _resident((D_MODEL, D_MODEL)), _resident((1, D_MODEL)), _resident((1, D_MODEL))],
        out_specs=_rows(tile, D_MODEL),
        out_shape=jax.ShapeDtypeStruct((rows, D_MODEL), f32),
        compiler_params=_params("arbitrary"),
        name="conv_glu",
    )(x, g, wa, wb, ba, bb)


def _conv_b_prompt_body(u_ref, st_ref, w_ref, b_ref, c_ref, ns_ref, hist, cout):
    t = pl.program_id(1)
    tile = u_ref.shape[0]

    @pl.when(t == 0)
    def _():
        for l in range(N_SLABS):
            hist[l, CONV_HALO - CONV_STATE:CONV_HALO, :] = st_ref[0, :, l * LANES:(l + 1) * LANES]

    @pl.when(t > 0)
    def _():
        for l in range(N_SLABS):
            hist[l, 0:CONV_HALO, :] = hist[l, tile:tile + CONV_HALO, :]

    u = u_ref[...]
    first_tap = CONV_HALO - CONV_STATE
    for l in range(N_SLABS):
        lanes = slice(l * LANES, (l + 1) * LANES)
        hist[l, CONV_HALO:CONV_HALO + tile, :] = u[:, lanes]
        taps = [jnp.broadcast_to(w_ref[j:j + 1, lanes], (SUBLANES, LANES)) for j in range(CONV_WIDTH)]

        def block(blk, carry, l=l, taps=taps):
            base = blk * CONV_BLOCK
            acc = [None] * CONV_STRIDE
            for shift in range(first_tap, first_tap + CONV_WIDTH + CONV_STRIDE - 1):
                rows = hist[l, pl.ds(base + shift, SUBLANES, stride=CONV_STRIDE), :]
                for t0 in range(CONV_STRIDE):
                    j = shift - t0 - first_tap
                    if 0 <= j < CONV_WIDTH:
                        term = rows * taps[j]
                        acc[t0] = term if acc[t0] is None else acc[t0] + term
            for t0 in range(CONV_STRIDE):
                cout[l, pl.ds(base + t0, SUBLANES, stride=CONV_STRIDE), :] = acc[t0]
            return carry

        lax.fori_loop(0, tile // CONV_BLOCK, block, 0)

    c_ref[...] = jnp.concatenate([cout[l] for l in range(N_SLABS)], axis=1) + b_ref[...]
    ns_ref[0] = jnp.concatenate(
        [hist[l, tile + CONV_HALO - CONV_STATE:tile + CONV_HALO, :] for l in range(N_SLABS)], axis=1)


def _conv_b_prompt(u, states, layer, w, b, batch, seq):
    tile = CONV_TILE
    nt = seq // tile
    return pl.pallas_call(
        _conv_b_prompt_body,
        grid=(batch, nt),
        in_specs=[pl.BlockSpec((tile, D_MODEL), lambda i, t: (i * nt + t, 0)),
                  pl.BlockSpec((None, 1, CONV_STATE, D_MODEL), lambda i, t: (layer, i, 0, 0)),
                  _resident((CONV_WIDTH, D_MODEL)), _resident((1, D_MODEL))],
        out_specs=[pl.BlockSpec((tile, D_MODEL), lambda i, t: (i * nt + t, 0)),
                   pl.BlockSpec((1, CONV_STATE, D_MODEL), lambda i, t: (i, 0, 0))],
        out_shape=[jax.ShapeDtypeStruct((batch * seq, D_MODEL), f32),
                   jax.ShapeDtypeStruct((batch, CONV_STATE, D_MODEL), f32)],
        scratch_shapes=[pltpu.VMEM((N_SLABS, CONV_HALO + tile, LANES), f32),
                        pltpu.VMEM((N_SLABS, tile, LANES), f32)],
        compiler_params=_params("arbitrary", "arbitrary"),
        name="conv_dw_prompt",
    )(u, states, w, b)


def _conv_b_sample_body(u_ref, st_ref, w_ref, b_ref, c_ref, ns_ref):
    steps = u_ref.shape[1]
    full = jnp.concatenate([st_ref[...], u_ref[...]], axis=1)
    w = w_ref[...][None]
    for t in range(steps):
        c_ref[:, t, :] = jnp.sum(full[:, t:t + CONV_WIDTH, :] * w, axis=1) + b_ref[...]
    ns_ref[...] = full[:, steps:steps + CONV_STATE, :]


def _conv_b_sample(u3, states, layer, w, b):
    batch, steps, _ = u3.shape
    bb = CONV_SAMPLE_BATCH
    return pl.pallas_call(
        _conv_b_sample_body,
        grid=(batch // bb,),
        in_specs=[pl.BlockSpec((bb, steps, D_MODEL), lambda i: (i, 0, 0)),
                  pl.BlockSpec((None, bb, CONV_STATE, D_MODEL), lambda i: (layer, i, 0, 0)),
                  _resident((CONV_WIDTH, D_MODEL)), _resident((1, D_MODEL))],
        out_specs=[pl.BlockSpec((bb, steps, D_MODEL), lambda i: (i, 0, 0)),
                   pl.BlockSpec((bb, CONV_STATE, D_MODEL), lambda i: (i, 0, 0))],
        out_shape=[jax.ShapeDtypeStruct((batch, steps, D_MODEL), f32),
                   jax.ShapeDtypeStruct((batch, CONV_STATE, D_MODEL), f32)],
        compiler_params=_params("arbitrary"),
        name="conv_dw_sample",
    )(u3, states, w, b)


def _conv_c_body(c_ref, x_ref, lg_ref, lb_ref, w2_ref, b2_ref, o_ref):
    c = c_ref[...]
    d = c - jnp.mean(c, axis=-1, keepdims=True)
    var = jnp.mean(d * d, axis=-1, keepdims=True)
    y = _silu(d * lax.rsqrt(var + EPS) * lg_ref[...] + lb_ref[...])
    o_ref[...] = x_ref[...] + _dot(y.astype(bf16), w2_ref[...]) + b2_ref[...]


def _conv_c(c, x, lg, lb, w2, b2):
    rows = x.shape[0]
    tile = min(ROW_TILE, rows)
    return pl.pallas_call(
        _conv_c_body,
        grid=(rows // tile,),
        in_specs=[_rows(tile, D_MODEL), _rows(tile, D_MODEL), _resident((1, D_MODEL)), _resident((1, D_MODEL)),
                  _resident((D_MODEL, D_MODEL)), _resident((1, D_MODEL))],
        out_specs=_rows(tile, D_MODEL),
        out_shape=jax.ShapeDtypeStruct((rows, D_MODEL), f32),
        compiler_params=_params("arbitrary"),
        name="conv_out",
    )(c, x, lg, lb, w2, b2)


def _gla_a_body(x_ref, g_ref, wq_ref, wk_ref, wv_ref, wr_ref, wz_ref, wg2_ref, bg_ref,
                q_ref, k_ref, v_ref, r_ref, la_ref):
    h = _rms(x_ref[...], g_ref[...]).astype(bf16)
    q_ref[...] = _dot(h, wq_ref[...]) * (GLA_DK ** -0.5)
    k_ref[...] = _dot(h, wk_ref[...])
    v_ref[...] = _dot(h, wv_ref[...])
    r_ref[...] = _dot(h, wr_ref[...])
    z = _dot(h, wz_ref[...])
    pre = _dot(z.astype(bf16), wg2_ref[...]) + bg_ref[...]
    la_ref[...] = (jnp.minimum(pre, 0.0) - jnp.log1p(jnp.exp(-jnp.abs(pre)))) * (1.0 / GLA_TAU)


def _gla_a(x, g, wq, wk, wv, wr, wz, wg2, bg):
    rows = x.shape[0]
    tile = min(ROW_TILE, rows)
    widths = (GLA_DKT, GLA_DKT, GLA_DVT, GLA_DVT, GLA_DKT)
    return pl.pallas_call(
        _gla_a_body,
        grid=(rows // tile,),
        in_specs=[_rows(tile, D_MODEL), _resident((1, D_MODEL)), _resident((D_MODEL, GLA_DKT)),
                  _resident((D_MODEL, GLA_DKT)), _resident((D_MODEL, GLA_DVT)), _resident((D_MODEL, GLA_DVT)),
                  _resident((D_MODEL, LANES)), _resident((LANES, GLA_DKT)), _resident((1, GLA_DKT))],
        out_specs=[_rows(tile, n) for n in widths],
        out_shape=[jax.ShapeDtypeStruct((rows, n), f32) for n in widths],
        compiler_params=_params("arbitrary"),
        name="gla_proj",
    )(x, g, wq, wk, wv, wr, wz, wg2, bg)


def _split_bf16(x):
    hi = x.astype(bf16)
    return hi, (x - hi.astype(f32)).astype(bf16)


def _gla_b_prompt_body(q_ref, k_ref, v_ref, la_ref, s0_ref, o_ref, so_ref, state, scores):
    c = pl.program_id(1)
    chunk = q_ref.shape[1]

    @pl.when(c == 0)
    def _():
        state[...] = s0_ref[0]

    la = la_ref[0]
    la_hi, la_lo = _split_bf16(la)
    tri = (lax.broadcasted_iota(jnp.int32, (chunk, chunk), 1)
           <= lax.broadcasted_iota(jnp.int32, (chunk, chunk), 0)).astype(bf16)
    b_all = _dot(tri, la_hi) + _dot(tri, la_lo)
    ones = jnp.ones((chunk, LANES), bf16)

    for h in range(GLA_HEADS):
        ks = slice(h * GLA_DK, (h + 1) * GLA_DK)
        vs = slice(h * GLA_DV, (h + 1) * GLA_DV)
        b = b_all[:, ks]
        qh = q_ref[0, :, ks]
        kh = k_ref[0, :, ks]
        vb = v_ref[0, :, vs].astype(bf16)
        b_last = b[chunk - 1:chunk, :]
        q_dec = (qh * jnp.exp(b)).astype(bf16)
        k_end = (kh * jnp.exp(b_last - b)).astype(bf16)
        s_old = state[h]
        o_inter = _dot(q_dec, s_old.astype(bf16))

        for i in range(chunk // GLA_SUB):
            r0 = i * GLA_SUB
            nk = r0 + GLA_SUB
            npad = LANES * (-(-nk // LANES))
            bq = b[r0:nk]
            bk = b[0:nk]
            if i == 0:
                q_exp, k_exp = bq, -bk
            else:
                ref = b[r0 - 1:r0, :]
                q_exp, k_exp = bq - ref, ref - bk
            qi = (qh[r0:nk] * jnp.exp(q_exp)).astype(bf16)
            ki = (kh[0:nk] * jnp.exp(k_exp)).astype(bf16)
            if nk < npad:
                ki = jnp.concatenate([ki, jnp.zeros((npad - nk, GLA_DK), bf16)], axis=0)
            s = _dot_t_rhs(qi, ki)
            causal = (lax.broadcasted_iota(jnp.int32, (GLA_SUB, npad), 1)
                      <= lax.broadcasted_iota(jnp.int32, (GLA_SUB, npad), 0) + r0)
            scores[r0:nk, 0:npad] = jnp.where(causal, s, 0.0).astype(bf16)
            if npad < chunk:
                scores[r0:nk, npad:chunk] = jnp.zeros((GLA_SUB, chunk - npad), bf16)

        o_ref[0, :, vs] = o_inter + _dot(scores[...], vb)

        dcol = _dot_t_lhs(la_hi[:, ks], ones) + _dot_t_lhs(la_lo[:, ks], ones)
        decay = jnp.exp(dcol)
        decay = jnp.concatenate([decay] * (GLA_DV // LANES), axis=1)
        state[h] = s_old * decay + _dot_t_lhs(k_end, vb)

    @pl.when(c == pl.num_programs(1) - 1)
    def _():
        so_ref[0] = state[...]


def _gla_b_prompt(q, k, v, la, states, layer):
    batch, seq, _ = q.shape
    chunk = GLA_CHUNK
    seq_block = lambda n: pl.BlockSpec((1, chunk, n), lambda i, c: (i, c, 0))
    state_block = pl.BlockSpec((1, GLA_HEADS, GLA_DK, GLA_DV), lambda i, c: (i, 0, 0, 0))
    state_in = pl.BlockSpec((None, 1, GLA_HEADS, GLA_DK, GLA_DV), lambda i, c: (layer, i, 0, 0, 0))
    return pl.pallas_call(
        _gla_b_prompt_body,
        grid=(batch, seq // chunk),
        in_specs=[seq_block(GLA_DKT), seq_block(GLA_DKT), seq_block(GLA_DVT), seq_block(GLA_DKT), state_in],
        out_specs=[seq_block(GLA_DVT), state_block],
        out_shape=[jax.ShapeDtypeStruct((batch, seq, GLA_DVT), f32),
                   jax.ShapeDtypeStruct((batch, GLA_HEADS, GLA_DK, GLA_DV), f32)],
        scratch_shapes=[pltpu.VMEM((GLA_HEADS, GLA_DK, GLA_DV), f32), pltpu.VMEM((chunk, chunk), bf16)],
        compiler_params=_params("arbitrary", "arbitrary"),
        name="gla_core_prompt",
    )(q, k, v, la, states)


def _gla_b_sample_body(q_ref, k_ref, v_ref, la_ref, s0_ref, o_ref, so_ref):
    bb, steps, _ = q_ref.shape
    causal = (lax.broadcasted_iota(jnp.int32, (steps, steps), 1)
              <= lax.broadcasted_iota(jnp.int32, (steps, steps), 0))
    ones = jnp.ones((steps, LANES), bf16)

    def one_sequence(i, carry):
        la = la_ref[i]
        cum = [la[0:1]]
        for t in range(1, steps):
            cum.append(cum[-1] + la[t:t + 1])
        b_all = jnp.concatenate(cum, axis=0)
        la_hi, la_lo = _split_bf16(la)
        for h in range(GLA_HEADS):
            ks = slice(h * GLA_DK, (h + 1) * GLA_DK)
            vs = slice(h * GLA_DV, (h + 1) * GLA_DV)
            b = b_all[:, ks]
            qh = q_ref[i, :, ks]
            kh = k_ref[i, :, ks]
            vb = v_ref[i, :, vs].astype(bf16)
            b_last = b[steps - 1:steps, :]
            q_dec = (qh * jnp.exp(b)).astype(bf16)
            k_inv = (kh * jnp.exp(-b)).astype(bf16)
            k_end = (kh * jnp.exp(b_last - b)).astype(bf16)
            s_old = s0_ref[i, h]
            s = jnp.where(causal, _dot_t_rhs(q_dec, k_inv), 0.0).astype(bf16)
            o_ref[i, :, vs] = _dot(s, vb) + _dot(q_dec, s_old.astype(bf16))
            dcol = _dot_t_lhs(la_hi[:, ks], ones) + _dot_t_lhs(la_lo[:, ks], ones)
            decay = jnp.exp(dcol)
            decay = jnp.concatenate([decay] * (GLA_DV // LANES), axis=1)
            so_ref[i, h] = s_old * decay + _dot_t_lhs(k_end, vb)
        return carry

    lax.fori_loop(0, bb, one_sequence, 0)


def _gla_b_sample(q, k, v, la, states, layer):
    batch, steps, _ = q.shape
    bb = GLA_SAMPLE_BATCH
    seq_block = lambda n: pl.BlockSpec((bb, steps, n), lambda i: (i, 0, 0))
    state_block = pl.BlockSpec((bb, GLA_HEADS, GLA_DK, GLA_DV), lambda i: (i, 0, 0, 0))
    state_in = pl.BlockSpec((None, bb, GLA_HEADS, GLA_DK, GLA_DV), lambda i: (layer, i, 0, 0, 0))
    return pl.pallas_call(
        _gla_b_sample_body,
        grid=(batch // bb,),
        in_specs=[seq_block(GLA_DKT), seq_block(GLA_DKT), seq_block(GLA_DVT), seq_block(GLA_DKT), state_in],
        out_specs=[seq_block(GLA_DVT), state_block],
        out_shape=[jax.ShapeDtypeStruct((batch, steps, GLA_DVT), f32),
                   jax.ShapeDtypeStruct((batch, GLA_HEADS, GLA_DK, GLA_DV), f32)],
        compiler_params=_params("arbitrary"),
        name="gla_core_sample",
    )(q, k, v, la, states)


def _gla_c_body(o_ref, r_ref, x_ref, go_ref, wo_ref, out_ref):
    parts = []
    for h in range(GLA_HEADS):
        vs = slice(h * GLA_DV, (h + 1) * GLA_DV)
        parts.append((_rms(o_ref[:, vs], go_ref[...]) * _silu(r_ref[:, vs])).astype(bf16))
    out_ref[...] = x_ref[...] + _dot(jnp.concatenate(parts, axis=1), wo_ref[...])


def _gla_c(o, r, x, go, wo):
    rows = x.shape[0]
    tile = min(ROW_TILE, rows)
    return pl.pallas_call(
        _gla_c_body,
        grid=(rows // tile,),
        in_specs=[_rows(tile, GLA_DVT), _rows(tile, GLA_DVT), _rows(tile, D_MODEL), _resident((1, GLA_DV)),
                  _resident((GLA_DVT, D_MODEL))],
        out_specs=_rows(tile, D_MODEL),
        out_shape=jax.ShapeDtypeStruct((rows, D_MODEL), f32),
        compiler_params=_params("arbitrary"),
        name="gla_out",
    )(o, r, x, go, wo)


def _trunk(x3, conv_state, gla_state, w, *, prompt):
    batch, seq, _ = x3.shape
    x = x3.reshape(batch * seq, D_MODEL)
    new_conv, new_gla = [], []
    for i in range(DEPTH):
        j = i // 2
        x = _ffn(x, w["norm_ffn_pre"][i], w["ffn_w_gate"][i][0], w["ffn_w_up"][i][0], w["ffn_w_down"][i][0],
                 w["norm_final"], final_norm=False)
        if i % 2 == 0:
            u = _conv_a(x, w["norm_mix"][i], w["conv_w_a"][j], w["conv_w_b"][j], w["conv_b_a"][j], w["conv_b_b"][j])
            if prompt:
                c, s = _conv_b_prompt(u, conv_state, j, w["conv_w_dw"][j], w["conv_b_dw"][j], batch, seq)
            else:
                c, s = _conv_b_sample(u.reshape(batch, seq, D_MODEL), conv_state, j, w["conv_w_dw"][j],
                                      w["conv_b_dw"][j])
                c = c.reshape(batch * seq, D_MODEL)
            new_conv.append(s)
            x = _conv_c(c, x, w["conv_ln_g"][j], w["conv_ln_b"][j], w["conv_w_pw2"][j], w["conv_b_pw2"][j])
        else:
            q, k, v, r, la = _gla_a(x, w["norm_mix"][i], w["gla_w_q"][j], w["gla_w_k"][j], w["gla_w_v"][j],
                                    w["gla_w_r"][j], w["gla_w_z"][j], w["gla_w_gate2"][j], w["gla_b_gate"][j])
            shape3 = lambda a: a.reshape(batch, seq, a.shape[-1])
            core = _gla_b_prompt if prompt else _gla_b_sample
            o, s = core(shape3(q), shape3(k), shape3(v), shape3(la), gla_state, j)
            new_gla.append(s)
            x = _gla_c(o.reshape(batch * seq, GLA_DVT), r, x, w["gla_onorm_g"][j], w["gla_w_out"][j])
        x = _ffn(x, w["norm_ffn_post"][i], w["ffn_w_gate"][i][1], w["ffn_w_up"][i][1], w["ffn_w_down"][i][1],
                 w["norm_final"], final_norm=(i == DEPTH - 1))
    return x.reshape(batch, seq, D_MODEL), jnp.stack(new_conv), jnp.stack(new_gla)


def kernel(x_prompt, x_sample, state_conv, state_gla, norm_ffn_pre, norm_mix, norm_ffn_post, norm_final,
           ffn_w_gate, ffn_w_up, ffn_w_down, conv_w_pw1, conv_b_pw1, conv_w_dw, conv_b_dw, conv_ln_g, conv_ln_b,
           conv_w_pw2, conv_b_pw2, gla_w_in, gla_w_gate2, gla_b_gate, gla_onorm_g, gla_w_out):
    row = lambda a: a[..., None, :]
    q_end, k_end, v_end, r_end = GLA_DKT, 2 * GLA_DKT, 2 * GLA_DKT + GLA_DVT, 2 * GLA_DKT + 2 * GLA_DVT
    w_z = jnp.pad(gla_w_in[:, :, r_end:], ((0, 0), (0, 0), (0, LANES - GLA_GATE_RANK)))
    w_gate2 = jnp.pad(gla_w_gate2, ((0, 0), (0, LANES - GLA_GATE_RANK), (0, 0)))
    w = dict(
        norm_ffn_pre=row(norm_ffn_pre), norm_mix=row(norm_mix), norm_ffn_post=row(norm_ffn_post),
        norm_final=norm_final[None, :],
        ffn_w_gate=ffn_w_gate.astype(bf16), ffn_w_up=ffn_w_up.astype(bf16), ffn_w_down=ffn_w_down.astype(bf16),
        conv_w_a=conv_w_pw1[:, :, :D_MODEL].astype(bf16), conv_w_b=conv_w_pw1[:, :, D_MODEL:].astype(bf16),
        conv_b_a=row(conv_b_pw1[:, :D_MODEL]), conv_b_b=row(conv_b_pw1[:, D_MODEL:]),
        conv_w_dw=conv_w_dw, conv_b_dw=row(conv_b_dw), conv_ln_g=row(conv_ln_g), conv_ln_b=row(conv_ln_b),
        conv_w_pw2=conv_w_pw2.astype(bf16), conv_b_pw2=row(conv_b_pw2),
        gla_w_q=gla_w_in[:, :, :q_end].astype(bf16), gla_w_k=gla_w_in[:, :, q_end:k_end].astype(bf16),
        gla_w_v=gla_w_in[:, :, k_end:v_end].astype(bf16), gla_w_r=gla_w_in[:, :, v_end:r_end].astype(bf16),
        gla_w_z=w_z.astype(bf16), gla_w_gate2=w_gate2.astype(bf16), gla_b_gate=row(gla_b_gate),
        gla_onorm_g=row(gla_onorm_g), gla_w_out=gla_w_out.astype(bf16),
    )
    batch = x_prompt.shape[0]
    conv0 = jnp.zeros((state_conv.shape[0], batch) + state_conv.shape[2:], x_prompt.dtype)
    gla0 = jnp.zeros((state_gla.shape[0], batch) + state_gla.shape[2:], x_prompt.dtype)
    y_p, conv_p, gla_p = _trunk(x_prompt, conv0, gla0, w, prompt=True)
    y_s, conv_s, gla_s = _trunk(x_sample, state_conv, state_gla, w, prompt=False)
    return (y_p, y_s, conv_p, gla_p, conv_s, gla_s)
```

```python
import functools

import jax
import jax.numpy as jnp
from jax import lax
from jax.experimental import pallas as pl
from jax.experimental.pallas import tpu as pltpu

f32 = jnp.float32
bf16 = jnp.bfloat16

D_MODEL = 1024
D_FF = 2816
DEPTH = 4
CONV_WIDTH = 31
CONV_STATE = CONV_WIDTH - 1
GLA_HEADS = 4
GLA_DK = 128
GLA_DV = 256
GLA_DKT = GLA_HEADS * GLA_DK
GLA_DVT = GLA_HEADS * GLA_DV
GLA_GATE_RANK = 16
GLA_TAU = 16.0
EPS = 1e-6

LANES = 128
SUBLANES = 8
N_SLABS = D_MODEL // LANES
VMEM_LIMIT = 56 * 1024 * 1024

ROW_TILE = 512
FFN_CHUNK = 256
CONV_TILE = 512
CONV_HALO = 32
CONV_STRIDE = 4
CONV_BLOCK = SUBLANES * CONV_STRIDE
CONV_CHAINS = 3
CONV_SAMPLE_BATCH = 16
GLA_CHUNK = 256
GLA_SUB = 32
GLA_SAMPLE_BATCH = 8


def _params(*semantics):
    return pltpu.CompilerParams(dimension_semantics=semantics, vmem_limit_bytes=VMEM_LIMIT)


def _resident(block, index=None):
    index = (0,) * len(block) if index is None else index
    return pl.BlockSpec(block, lambda *_: index, pipeline_mode=pl.Buffered(1))


def _vec(layer, width=D_MODEL, part=0):
    return _resident((None, 1, width), (layer, 0, part))


def _mat(layer, rows, cols, part=0):
    return _resident((None, rows, cols), (layer, 0, part))


def _rows(tile, width):
    return pl.BlockSpec((tile, width), lambda i: (i, 0))


def _stacked(prev, in_count, out_index):
    if prev is None:
        return [], [], {}
    return [pl.BlockSpec(memory_space=pl.ANY)], [prev], {in_count: out_index}


def _rms(x, g):
    return x * lax.rsqrt(jnp.mean(x * x, axis=-1, keepdims=True) + EPS) * g


def _silu(x):
    return x * jax.nn.sigmoid(x)


def _dot(a, b):
    return jnp.dot(a, b, preferred_element_type=f32)


def _dot_t_rhs(a, b):
    return lax.dot_general(a, b, (((1,), (1,)), ((), ())), preferred_element_type=f32)


def _dot_t_lhs(a, b):
    return lax.dot_general(a, b, (((0,), (0,)), ((), ())), preferred_element_type=f32)


def _ffn_body(x_ref, g_ref, wg_ref, wu_ref, wd_ref, gf_ref, o_ref, a_scr, *, final_norm):
    x = x_ref[...]
    h = _rms(x, g_ref[...]).astype(bf16)
    for j in range(D_FF // FFN_CHUNK):
        cols = pl.ds(j * FFN_CHUNK, FFN_CHUNK)
        gate = _dot(h, wg_ref[:, cols])
        up = _dot(h, wu_ref[:, cols])
        a_scr[:, cols] = (_silu(gate) * up).astype(bf16)
    y = x + 0.5 * _dot(a_scr[...], wd_ref[...])
    if final_norm:
        y = _rms(y, gf_ref[...])
    o_ref[...] = y


def _ffn(x, w, gain, layer, which, *, final_norm=False):
    rows = x.shape[0]
    tile = min(ROW_TILE, rows)
    weight = lambda r, c: _resident((None, None, r, c), (layer, which, 0, 0))
    return pl.pallas_call(
        functools.partial(_ffn_body, final_norm=final_norm),
        grid=(rows // tile,),
        in_specs=[_rows(tile, D_MODEL), _vec(layer), weight(D_MODEL, D_FF), weight(D_MODEL, D_FF),
                  weight(D_FF, D_MODEL), _resident((1, D_MODEL))],
        out_specs=_rows(tile, D_MODEL),
        out_shape=jax.ShapeDtypeStruct((rows, D_MODEL), f32),
        scratch_shapes=[pltpu.VMEM((tile, D_FF), bf16)],
        compiler_params=_params("arbitrary"),
        name="ffn",
    )(x, w[gain], w["ffn_w_gate"], w["ffn_w_up"], w["ffn_w_down"], w["norm_final"])


def _conv_glu(x, g_ref, wa_ref, wb_ref, ba_ref, bb_ref):
    h = _rms(x, g_ref[...]).astype(bf16)
    a = _dot(h, wa_ref[...]) + ba_ref[...]
    gate = _dot(h, wb_ref[...]) + bb_ref[...]
    return a * jax.nn.sigmoid(gate)


def _conv_a_body(x_ref, g_ref, wa_ref, wb_ref, ba_ref, bb_ref, u_ref):
    u_ref[...] = _conv_glu(x_ref[...], g_ref, wa_ref, wb_ref, ba_ref, bb_ref)


def _conv_a(x, w, layer, j):
    rows = x.shape[0]
    tile = min(ROW_TILE, rows)
    return pl.pallas_call(
        _conv_a_body,
        grid=(rows // tile,),
        in_specs=[_rows(tile, D_MODEL), _vec(layer), _mat(j, D_MODEL, D_MODEL, 0), _mat(j, D_MODEL, D_MODEL, 1),
                  _vec(j, part=0), _vec(j, part=1)],
        out_specs=_rows(tile, D_MODEL),
        out_shape=jax.ShapeDtypeStruct((rows, D_MODEL), f32),
        compiler_params=_params("arbitrary"),
        name="conv_glu",
    )(x, w["norm_mix"], w["conv_w_pw1"], w["conv_w_pw1"], w["conv_b_pw1"], w["conv_b_pw1"])


def _conv_prompt_body(x_ref, g_ref, wa_ref, wb_ref, ba_ref, bb_ref, st_ref, w_ref, b_ref,
                      lg_ref, lb_ref, w2_ref, b2_ref, *rest):
    o_ref, ns_ref, hist, cout = rest[-4:]
    t = pl.program_id(1)
    tile = x_ref.shape[0]

    @pl.when(t == 0)
    def _():
        for l in range(N_SLABS):
            hist[l, CONV_HALO - CONV_STATE:CONV_HALO, :] = st_ref[0, :, l * LANES:(l + 1) * LANES]

    @pl.when(t > 0)
    def _():
        for l in range(N_SLABS):
            hist[l, 0:CONV_HALO, :] = hist[l, tile:tile + CONV_HALO, :]

    x = x_ref[...]
    u = _conv_glu(x, g_ref, wa_ref, wb_ref, ba_ref, bb_ref)
    first_tap = CONV_HALO - CONV_STATE
    for l in range(N_SLABS):
        lanes = slice(l * LANES, (l + 1) * LANES)
        hist[l, CONV_HALO:CONV_HALO + tile, :] = u[:, lanes]
        taps = [jnp.broadcast_to(w_ref[j:j + 1, lanes], (SUBLANES, LANES)) for j in range(CONV_WIDTH)]

        def block(blk, carry, l=l, taps=taps):
            base = blk * CONV_BLOCK
            acc = [[None] * CONV_CHAINS for _ in range(CONV_STRIDE)]
            for shift in range(first_tap, first_tap + CONV_WIDTH + CONV_STRIDE - 1):
                rows = hist[l, pl.ds(base + shift, SUBLANES, stride=CONV_STRIDE), :]
                for t0 in range(CONV_STRIDE):
                    j = shift - t0 - first_tap
                    if 0 <= j < CONV_WIDTH:
                        term = rows * taps[j]
                        chain = j % CONV_CHAINS
                        acc[t0][chain] = term if acc[t0][chain] is None else acc[t0][chain] + term
            for t0 in range(CONV_STRIDE):
                cout[l, pl.ds(base + t0, SUBLANES, stride=CONV_STRIDE), :] = functools.reduce(jnp.add, acc[t0])
            return carry

        lax.fori_loop(0, tile // CONV_BLOCK, block, 0)

    c = jnp.concatenate([cout[l] for l in range(N_SLABS)], axis=1) + b_ref[...]
    o_ref[...] = _conv_tail(c, x, lg_ref, lb_ref, w2_ref, b2_ref)
    ns_ref[0] = jnp.concatenate(
        [hist[l, tile + CONV_HALO - CONV_STATE:tile + CONV_HALO, :] for l in range(N_SLABS)], axis=1)


def _conv_prompt(x, states, w, layer, j, prev, batch, seq):
    tile = CONV_TILE
    nt = seq // tile
    n_layers = states.shape[0]
    extra_specs, extra_args, aliases = _stacked(prev, 13, 1)
    seq_tile = pl.BlockSpec((tile, D_MODEL), lambda i, t: (i * nt + t, 0))
    state_block = pl.BlockSpec((None, 1, CONV_STATE, D_MODEL), lambda i, t: (j, i, 0, 0))
    return pl.pallas_call(
        _conv_prompt_body,
        grid=(batch, nt),
        in_specs=[seq_tile, _vec(layer), _mat(j, D_MODEL, D_MODEL, 0), _mat(j, D_MODEL, D_MODEL, 1),
                  _vec(j, part=0), _vec(j, part=1), state_block, _mat(j, CONV_WIDTH, D_MODEL), _vec(j),
                  _vec(j), _vec(j), _mat(j, D_MODEL, D_MODEL), _vec(j)] + extra_specs,
        out_specs=[seq_tile, state_block],
        out_shape=[jax.ShapeDtypeStruct((batch * seq, D_MODEL), f32),
                   jax.ShapeDtypeStruct((n_layers, batch, CONV_STATE, D_MODEL), f32)],
        scratch_shapes=[pltpu.VMEM((N_SLABS, CONV_HALO + tile, LANES), f32),
                        pltpu.VMEM((N_SLABS, tile, LANES), f32)],
        input_output_aliases=aliases,
        compiler_params=_params("arbitrary", "arbitrary"),
        name="conv_prompt",
    )(x, w["norm_mix"], w["conv_w_pw1"], w["conv_w_pw1"], w["conv_b_pw1"], w["conv_b_pw1"], states,
      w["conv_w_dw"], w["conv_b_dw"], w["conv_ln_g"], w["conv_ln_b"], w["conv_w_pw2"], w["conv_b_pw2"],
      *extra_args)


def _conv_b_sample_body(u_ref, st_ref, w_ref, b_ref, *rest):
    c_ref, ns_ref = rest[-2:]
    steps = u_ref.shape[1]
    full = jnp.concatenate([st_ref[...], u_ref[...]], axis=1)
    taps = w_ref[...][None]
    for t in range(steps):
        c_ref[:, t, :] = jnp.sum(full[:, t:t + CONV_WIDTH, :] * taps, axis=1) + b_ref[...]
    ns_ref[...] = full[:, steps:steps + CONV_STATE, :]


def _conv_b_sample(u3, states, w, j, prev):
    batch, steps, _ = u3.shape
    bb = CONV_SAMPLE_BATCH
    n_layers = states.shape[0]
    extra_specs, extra_args, aliases = _stacked(prev, 4, 1)
    state_block = pl.BlockSpec((None, bb, CONV_STATE, D_MODEL), lambda i: (j, i, 0, 0))
    return pl.pallas_call(
        _conv_b_sample_body,
        grid=(batch // bb,),
        in_specs=[pl.BlockSpec((bb, steps, D_MODEL), lambda i: (i, 0, 0)), state_block,
                  _mat(j, CONV_WIDTH, D_MODEL), _vec(j)] + extra_specs,
        out_specs=[pl.BlockSpec((bb, steps, D_MODEL), lambda i: (i, 0, 0)), state_block],
        out_shape=[jax.ShapeDtypeStruct((batch, steps, D_MODEL), f32),
                   jax.ShapeDtypeStruct((n_layers, batch, CONV_STATE, D_MODEL), f32)],
        input_output_aliases=aliases,
        compiler_params=_params("arbitrary"),
        name="conv_dw_sample",
    )(u3, states, w["conv_w_dw"], w["conv_b_dw"], *extra_args)


def _conv_tail(c, x, lg_ref, lb_ref, w2_ref, b2_ref):
    d = c - jnp.mean(c, axis=-1, keepdims=True)
    var = jnp.mean(d * d, axis=-1, keepdims=True)
    y = _silu(d * lax.rsqrt(var + EPS) * lg_ref[...] + lb_ref[...])
    return x + _dot(y.astype(bf16), w2_ref[...]) + b2_ref[...]


def _conv_c_body(c_ref, x_ref, lg_ref, lb_ref, w2_ref, b2_ref, o_ref):
    o_ref[...] = _conv_tail(c_ref[...], x_ref[...], lg_ref, lb_ref, w2_ref, b2_ref)


def _conv_c(c, x, w, j):
    rows = x.shape[0]
    tile = min(ROW_TILE, rows)
    return pl.pallas_call(
        _conv_c_body,
        grid=(rows // tile,),
        in_specs=[_rows(tile, D_MODEL), _rows(tile, D_MODEL), _vec(j), _vec(j), _mat(j, D_MODEL, D_MODEL), _vec(j)],
        out_specs=_rows(tile, D_MODEL),
        out_shape=jax.ShapeDtypeStruct((rows, D_MODEL), f32),
        compiler_params=_params("arbitrary"),
        name="conv_out",
    )(c, x, w["conv_ln_g"], w["conv_ln_b"], w["conv_w_pw2"], w["conv_b_pw2"])


def _gla_a_body(x_ref, g_ref, wq_ref, wk_ref, wv_ref, wr_ref, wz_ref, wg2_ref, bg_ref,
                q_ref, k_ref, v_ref, r_ref, la_ref):
    q, k, v, r, la = _gla_project(x_ref[...], g_ref, wq_ref, wk_ref, wv_ref, wr_ref, wz_ref, wg2_ref, bg_ref)
    q_ref[...] = q
    k_ref[...] = k
    v_ref[...] = v
    r_ref[...] = r
    la_ref[...] = la


def _gla_project(x, g_ref, wq_ref, wk_ref, wv_ref, wr_ref, wz_ref, wg2_ref, bg_ref):
    h = _rms(x, g_ref[...]).astype(bf16)
    q = _dot(h, wq_ref[...]) * (GLA_DK ** -0.5)
    k = _dot(h, wk_ref[...])
    v = _dot(h, wv_ref[...])
    r = _dot(h, wr_ref[...])
    z = _dot(h, wz_ref[...])
    pre = _dot(z.astype(bf16), wg2_ref[...]) + bg_ref[...]
    la = (jnp.minimum(pre, 0.0) - jnp.log1p(jnp.exp(-jnp.abs(pre)))) * (1.0 / GLA_TAU)
    return q, k, v, r, la


def _gla_a(x, w, layer, j):
    rows = x.shape[0]
    tile = min(ROW_TILE, rows)
    widths = (GLA_DKT, GLA_DKT, GLA_DVT, GLA_DVT, GLA_DKT)
    return pl.pallas_call(
        _gla_a_body,
        grid=(rows // tile,),
        in_specs=[_rows(tile, D_MODEL), _vec(layer),
                  _mat(j, D_MODEL, GLA_DKT, 0), _mat(j, D_MODEL, GLA_DKT, 1),
                  _mat(j, D_MODEL, GLA_DVT, 1), _mat(j, D_MODEL, GLA_DVT, 2),
                  _mat(j, D_MODEL, LANES), _mat(j, LANES, GLA_DKT), _vec(j, GLA_DKT)],
        out_specs=[_rows(tile, n) for n in widths],
        out_shape=[jax.ShapeDtypeStruct((rows, n), f32) for n in widths],
        compiler_params=_params("arbitrary"),
        name="gla_proj",
    )(x, w["norm_mix"], w["gla_w_in"], w["gla_w_in"], w["gla_w_in"], w["gla_w_in"], w["gla_w_z"],
      w["gla_w_gate2"], w["gla_b_gate"])


def _split_bf16(x):
    hi = x.astype(bf16)
    return hi, (x - hi.astype(f32)).astype(bf16)


def _decay_columns(la_hi, la_lo, ones):
    total = _dot_t_lhs(la_hi, ones) + _dot_t_lhs(la_lo, ones)
    return jnp.concatenate([jnp.exp(total)] * (GLA_DV // LANES), axis=1)


def _gla_prompt_body(x_ref, g_ref, wq_ref, wk_ref, wv_ref, wr_ref, wz_ref, wg2_ref, bg_ref, s0_ref,
                     go_ref, wo_ref, *rest):
    out_ref, so_ref, state, scores = rest[-4:]
    c = pl.program_id(1)
    chunk = x_ref.shape[0]

    @pl.when(c == 0)
    def _():
        state[...] = s0_ref[0]

    x = x_ref[...]
    q, k, v, r, la = _gla_project(x, g_ref, wq_ref, wk_ref, wv_ref, wr_ref, wz_ref, wg2_ref, bg_ref)
    la_hi, la_lo = _split_bf16(la)
    tri = (lax.broadcasted_iota(jnp.int32, (chunk, chunk), 1)
           <= lax.broadcasted_iota(jnp.int32, (chunk, chunk), 0)).astype(bf16)
    b_all = _dot(tri, la_hi) + _dot(tri, la_lo)
    ones = jnp.ones((chunk, LANES), bf16)

    gated = []
    for h in range(GLA_HEADS):
        ks = slice(h * GLA_DK, (h + 1) * GLA_DK)
        vs = slice(h * GLA_DV, (h + 1) * GLA_DV)
        b = b_all[:, ks]
        qh = q[:, ks]
        kh = k[:, ks]
        vb = v[:, vs].astype(bf16)
        b_last = b[chunk - 1:chunk, :]
        q_dec = (qh * jnp.exp(b)).astype(bf16)
        k_end = (kh * jnp.exp(b_last - b)).astype(bf16)
        s_old = state[h]
        o_inter = _dot(q_dec, s_old.astype(bf16))

        for i in range(chunk // GLA_SUB):
            r0 = i * GLA_SUB
            nk = r0 + GLA_SUB
            npad = LANES * (-(-nk // LANES))
            bq = b[r0:nk]
            bk = b[0:nk]
            if i == 0:
                q_exp, k_exp = bq, -bk
            else:
                ref = b[r0 - 1:r0, :]
                q_exp, k_exp = bq - ref, ref - bk
            qi = (qh[r0:nk] * jnp.exp(q_exp)).astype(bf16)
            ki = (kh[0:nk] * jnp.exp(k_exp)).astype(bf16)
            if nk < npad:
                ki = jnp.concatenate([ki, jnp.zeros((npad - nk, GLA_DK), bf16)], axis=0)
            s = _dot_t_rhs(qi, ki)
            causal = (lax.broadcasted_iota(jnp.int32, (GLA_SUB, npad), 1)
                      <= lax.broadcasted_iota(jnp.int32, (GLA_SUB, npad), 0) + r0)
            scores[r0:nk, 0:npad] = jnp.where(causal, s, 0.0).astype(bf16)
            if npad < chunk:
                scores[r0:nk, npad:chunk] = jnp.zeros((GLA_SUB, chunk - npad), bf16)

        o = o_inter + _dot(scores[...], vb)
        gated.append(_gla_gate(o, r[:, vs], go_ref))
        state[h] = s_old * _decay_columns(la_hi[:, ks], la_lo[:, ks], ones) + _dot_t_lhs(k_end, vb)

    out_ref[...] = x + _dot(jnp.concatenate(gated, axis=1), wo_ref[...])

    @pl.when(c == pl.num_programs(1) - 1)
    def _():
        so_ref[0] = state[...]


def _gla_prompt(x, states, w, layer, j, prev, batch, seq):
    chunk = GLA_CHUNK
    nc = seq // chunk
    n_layers = states.shape[0]
    extra_specs, extra_args, aliases = _stacked(prev, 12, 1)
    seq_tile = pl.BlockSpec((chunk, D_MODEL), lambda i, c: (i * nc + c, 0))
    state_block = pl.BlockSpec((None, 1, GLA_HEADS, GLA_DK, GLA_DV), lambda i, c: (j, i, 0, 0, 0))
    return pl.pallas_call(
        _gla_prompt_body,
        grid=(batch, nc),
        in_specs=[seq_tile, _vec(layer),
                  _mat(j, D_MODEL, GLA_DKT, 0), _mat(j, D_MODEL, GLA_DKT, 1),
                  _mat(j, D_MODEL, GLA_DVT, 1), _mat(j, D_MODEL, GLA_DVT, 2),
                  _mat(j, D_MODEL, LANES), _mat(j, LANES, GLA_DKT), _vec(j, GLA_DKT), state_block,
                  _vec(j, GLA_DV), _mat(j, GLA_DVT, D_MODEL)] + extra_specs,
        out_specs=[seq_tile, state_block],
        out_shape=[jax.ShapeDtypeStruct((batch * seq, D_MODEL), f32),
                   jax.ShapeDtypeStruct((n_layers, batch, GLA_HEADS, GLA_DK, GLA_DV), f32)],
        scratch_shapes=[pltpu.VMEM((GLA_HEADS, GLA_DK, GLA_DV), f32), pltpu.VMEM((chunk, chunk), bf16)],
        input_output_aliases=aliases,
        compiler_params=_params("arbitrary", "arbitrary"),
        name="gla_prompt",
    )(x, w["norm_mix"], w["gla_w_in"], w["gla_w_in"], w["gla_w_in"], w["gla_w_in"], w["gla_w_z"],
      w["gla_w_gate2"], w["gla_b_gate"], states, w["gla_onorm_g"], w["gla_w_out"], *extra_args)


def _gla_b_sample_body(q_ref, k_ref, v_ref, la_ref, s0_ref, *rest, steps):
    o_ref, so_ref = rest[-2:]
    per_group = SUBLANES // steps
    causal = (lax.broadcasted_iota(jnp.int32, (steps, steps), 1)
              <= lax.broadcasted_iota(jnp.int32, (steps, steps), 0))
    ones = jnp.ones((steps, LANES), bf16)

    def group(g, carry):
        rows = pl.ds(pl.multiple_of(g * SUBLANES, SUBLANES), SUBLANES)
        q8, k8, v8, la8 = q_ref[rows, :], k_ref[rows, :], v_ref[rows, :], la_ref[rows, :]
        outs = []
        for s in range(per_group):
            i = g * per_group + s
            mine = slice(s * steps, (s + 1) * steps)
            la = la8[mine]
            cum = [la[0:1]]
            for t in range(1, steps):
                cum.append(cum[-1] + la[t:t + 1])
            b_all = jnp.concatenate(cum, axis=0)
            la_hi, la_lo = _split_bf16(la)
            heads = []
            for h in range(GLA_HEADS):
                ks = slice(h * GLA_DK, (h + 1) * GLA_DK)
                vs = slice(h * GLA_DV, (h + 1) * GLA_DV)
                b = b_all[:, ks]
                qh = q8[mine, ks]
                kh = k8[mine, ks]
                vb = v8[mine, vs].astype(bf16)
                b_last = b[steps - 1:steps, :]
                q_dec = (qh * jnp.exp(b)).astype(bf16)
                k_inv = (kh * jnp.exp(-b)).astype(bf16)
                k_end = (kh * jnp.exp(b_last - b)).astype(bf16)
                s_old = s0_ref[i, h]
                sc = jnp.where(causal, _dot_t_rhs(q_dec, k_inv), 0.0).astype(bf16)
                heads.append(_dot(sc, vb) + _dot(q_dec, s_old.astype(bf16)))
                so_ref[i, h] = (s_old * _decay_columns(la_hi[:, ks], la_lo[:, ks], ones)
                                + _dot_t_lhs(k_end, vb))
            outs.append(jnp.concatenate(heads, axis=1))
        o_ref[rows, :] = jnp.concatenate(outs, axis=0)
        return carry

    lax.fori_loop(0, q_ref.shape[0] // SUBLANES, group, 0)


def _gla_b_sample(q, k, v, la, states, j, prev, batch, steps):
    bb = GLA_SAMPLE_BATCH
    n_layers = states.shape[0]
    extra_specs, extra_args, aliases = _stacked(prev, 5, 1)
    state_block = pl.BlockSpec((None, bb, GLA_HEADS, GLA_DK, GLA_DV), lambda i: (j, i, 0, 0, 0))
    return pl.pallas_call(
        functools.partial(_gla_b_sample_body, steps=steps),
        grid=(batch // bb,),
        in_specs=[_rows(bb * steps, GLA_DKT), _rows(bb * steps, GLA_DKT), _rows(bb * steps, GLA_DVT),
                  _rows(bb * steps, GLA_DKT), state_block] + extra_specs,
        out_specs=[_rows(bb * steps, GLA_DVT), state_block],
        out_shape=[jax.ShapeDtypeStruct((batch * steps, GLA_DVT), f32),
                   jax.ShapeDtypeStruct((n_layers, batch, GLA_HEADS, GLA_DK, GLA_DV), f32)],
        input_output_aliases=aliases,
        compiler_params=_params("arbitrary"),
        name="gla_core_sample",
    )(q, k, v, la, states, *extra_args)


def _gla_gate(o, r, go_ref):
    return (_rms(o, go_ref[...]) * _silu(r)).astype(bf16)


def _gla_c_body(o_ref, r_ref, x_ref, go_ref, wo_ref, out_ref):
    parts = []
    for h in range(GLA_HEADS):
        vs = slice(h * GLA_DV, (h + 1) * GLA_DV)
        parts.append(_gla_gate(o_ref[:, vs], r_ref[:, vs], go_ref))
    out_ref[...] = x_ref[...] + _dot(jnp.concatenate(parts, axis=1), wo_ref[...])


def _gla_c(o, r, x, w, j):
    rows = x.shape[0]
    tile = min(ROW_TILE, rows)
    return pl.pallas_call(
        _gla_c_body,
        grid=(rows // tile,),
        in_specs=[_rows(tile, GLA_DVT), _rows(tile, GLA_DVT), _rows(tile, D_MODEL), _vec(j, GLA_DV),
                  _mat(j, GLA_DVT, D_MODEL)],
        out_specs=_rows(tile, D_MODEL),
        out_shape=jax.ShapeDtypeStruct((rows, D_MODEL), f32),
        compiler_params=_params("arbitrary"),
        name="gla_out",
    )(o, r, x, w["gla_onorm_g"], w["gla_w_out"])


def _trunk(x3, conv_state, gla_state, w, *, prompt):
    batch, seq, _ = x3.shape
    x = x3.reshape(batch * seq, D_MODEL)
    new_conv, new_gla = None, None
    for i in range(DEPTH):
        j = i // 2
        x = _ffn(x, w, "norm_ffn_pre", i, 0)
        if i % 2 == 0 and prompt:
            x, new_conv = _conv_prompt(x, conv_state, w, i, j, new_conv, batch, seq)
        elif i % 2 == 0:
            u = _conv_a(x, w, i, j)
            c, new_conv = _conv_b_sample(u.reshape(batch, seq, D_MODEL), conv_state, w, j, new_conv)
            x = _conv_c(c.reshape(batch * seq, D_MODEL), x, w, j)
        elif prompt:
            x, new_gla = _gla_prompt(x, gla_state, w, i, j, new_gla, batch, seq)
        else:
            q, k, v, r, la = _gla_a(x, w, i, j)
            o, new_gla = _gla_b_sample(q, k, v, la, gla_state, j, new_gla, batch, seq)
            x = _gla_c(o, r, x, w, j)
        x = _ffn(x, w, "norm_ffn_post", i, 1, final_norm=(i == DEPTH - 1))
    return x.reshape(batch, seq, D_MODEL), new_conv, new_gla


def kernel(x_prompt, x_sample, state_conv, state_gla, norm_ffn_pre, norm_mix, norm_ffn_post, norm_final,
           ffn_w_gate, ffn_w_up, ffn_w_down, conv_w_pw1, conv_b_pw1, conv_w_dw, conv_b_dw, conv_ln_g, conv_ln_b,
           conv_w_pw2, conv_b_pw2, gla_w_in, gla_w_gate2, gla_b_gate, gla_onorm_g, gla_w_out):
    row = lambda a: a[..., None, :]
    z_start = 2 * GLA_DKT + 2 * GLA_DVT
    w_z = jnp.pad(gla_w_in[:, :, z_start:], ((0, 0), (0, 0), (0, LANES - GLA_GATE_RANK)))
    w_gate2 = jnp.pad(gla_w_gate2, ((0, 0), (0, LANES - GLA_GATE_RANK), (0, 0)))
    w = dict(
        norm_ffn_pre=row(norm_ffn_pre), norm_mix=row(norm_mix), norm_ffn_post=row(norm_ffn_post),
        norm_final=norm_final[None, :],
        ffn_w_gate=ffn_w_gate.astype(bf16), ffn_w_up=ffn_w_up.astype(bf16), ffn_w_down=ffn_w_down.astype(bf16),
        conv_w_pw1=conv_w_pw1.astype(bf16), conv_b_pw1=row(conv_b_pw1),
        conv_w_dw=conv_w_dw, conv_b_dw=row(conv_b_dw), conv_ln_g=row(conv_ln_g), conv_ln_b=row(conv_ln_b),
        conv_w_pw2=conv_w_pw2.astype(bf16), conv_b_pw2=row(conv_b_pw2),
        gla_w_in=gla_w_in.astype(bf16), gla_w_z=w_z.astype(bf16), gla_w_gate2=w_gate2.astype(bf16),
        gla_b_gate=row(gla_b_gate), gla_onorm_g=row(gla_onorm_g), gla_w_out=gla_w_out.astype(bf16),
    )
    batch = x_prompt.shape[0]
    conv0 = jnp.zeros((state_conv.shape[0], batch) + state_conv.shape[2:], x_prompt.dtype)
    gla0 = jnp.zeros((state_gla.shape[0], batch) + state_gla.shape[2:], x_prompt.dtype)
    y_p, conv_p, gla_p = _trunk(x_prompt, conv0, gla0, w, prompt=True)
    y_s, conv_s, gla_s = _trunk(x_sample, state_conv, state_gla, w, prompt=False)
    return (y_p, y_s, conv_p, gla_p, conv_s, gla_s)
```

```python
import functools

import jax
import jax.numpy as jnp
from jax import lax
from jax.experimental import pallas as pl
from jax.experimental.pallas import tpu as pltpu

f32 = jnp.float32
bf16 = jnp.bfloat16

D_MODEL = 1024
D_FF = 2816
DEPTH = 4
CONV_WIDTH = 31
CONV_STATE = CONV_WIDTH - 1
GLA_HEADS = 4
GLA_DK = 128
GLA_DV = 256
GLA_DKT = GLA_HEADS * GLA_DK
GLA_DVT = GLA_HEADS * GLA_DV
GLA_GATE_RANK = 16
GLA_TAU = 16.0
EPS = 1e-6

LANES = 128
SUBLANES = 8
N_SLABS = D_MODEL // LANES
VMEM_LIMIT = 56 * 1024 * 1024

ROW_TILE = 512
FFN_TILE = 1024
FFN_CHUNK = 256
CONV_TILE = 512
CONV_HALO = 32
CONV_STRIDE = 4
CONV_BLOCK = SUBLANES * CONV_STRIDE
CONV_CHAINS = 3
CONV_SAMPLE_BATCH = 8
GLA_CHUNK = 256
GLA_SEQS = 2
GLA_SUB = 32
GLA_SAMPLE_BATCH = 8


def _params(*semantics):
    return pltpu.CompilerParams(dimension_semantics=semantics, vmem_limit_bytes=VMEM_LIMIT)


def _resident(block, index=None):
    index = (0,) * len(block) if index is None else index
    return pl.BlockSpec(block, lambda *_: index, pipeline_mode=pl.Buffered(1))


def _vec(layer, width=D_MODEL, part=0):
    return _resident((None, 1, width), (layer, 0, part))


def _mat(layer, rows, cols, part=0):
    return _resident((None, rows, cols), (layer, 0, part))


def _rows(tile, width):
    return pl.BlockSpec((tile, width), lambda i: (i, 0))


def _stacked(prev, in_count, out_index):
    if prev is None:
        return [], [], {}
    return [pl.BlockSpec(memory_space=pl.ANY)], [prev], {in_count: out_index}


def _layer_block(first, n_layers, layer, block, index):
    lead, at = (n_layers, 0) if first else (None, layer)
    return pl.BlockSpec((lead,) + block, lambda *grid: (at,) + index(*grid))


def _own_layer(ref, layer, first):
    if not first:
        return ref
    for other in range(ref.shape[0]):
        if other != layer:
            ref[other] = jnp.zeros(ref.shape[1:], ref.dtype)
    return ref.at[layer]


def _rms(x, g):
    return x * lax.rsqrt(jnp.mean(x * x, axis=-1, keepdims=True) + EPS) * g


def _silu(x):
    return x * jax.nn.sigmoid(x)


def _dot(a, b):
    return jnp.dot(a, b, preferred_element_type=f32)


def _dot_t_rhs(a, b):
    return lax.dot_general(a, b, (((1,), (1,)), ((), ())), preferred_element_type=f32)


def _dot_t_lhs(a, b):
    return lax.dot_general(a, b, (((0,), (0,)), ((), ())), preferred_element_type=f32)


def _ffn_body(x_ref, g_ref, wg_ref, wu_ref, wd_ref, gf_ref, o_ref, a_scr, *, final_norm):
    x = x_ref[...]
    h = _rms(x, g_ref[...]).astype(bf16)
    for j in range(D_FF // FFN_CHUNK):
        cols = pl.ds(j * FFN_CHUNK, FFN_CHUNK)
        gate = _dot(h, wg_ref[:, cols])
        up = _dot(h, wu_ref[:, cols])
        a_scr[:, cols] = (_silu(gate) * up).astype(bf16)
    y = x + 0.5 * _dot(a_scr[...], wd_ref[...])
    if final_norm:
        y = _rms(y, gf_ref[...])
    o_ref[...] = y


def _ffn(x, w, gain, layer, which, *, final_norm=False):
    rows = x.shape[0]
    tile = min(FFN_TILE, rows)
    weight = lambda r, c: _resident((None, None, r, c), (layer, which, 0, 0))
    return pl.pallas_call(
        functools.partial(_ffn_body, final_norm=final_norm),
        grid=(rows // tile,),
        in_specs=[_rows(tile, D_MODEL), _vec(layer), weight(D_MODEL, D_FF), weight(D_MODEL, D_FF),
                  weight(D_FF, D_MODEL), _resident((1, D_MODEL))],
        out_specs=_rows(tile, D_MODEL),
        out_shape=jax.ShapeDtypeStruct((rows, D_MODEL), f32),
        scratch_shapes=[pltpu.VMEM((tile, D_FF), bf16)],
        compiler_params=_params("arbitrary"),
        name="ffn",
    )(x, w[gain], w["ffn_w_gate"], w["ffn_w_up"], w["ffn_w_down"], w["norm_final"])


def _conv_glu(x, g_ref, wa_ref, wb_ref, ba_ref, bb_ref):
    h = _rms(x, g_ref[...]).astype(bf16)
    a = _dot(h, wa_ref[...]) + ba_ref[...]
    gate = _dot(h, wb_ref[...]) + bb_ref[...]
    return a * jax.nn.sigmoid(gate)


def _conv_a_body(x_ref, g_ref, wa_ref, wb_ref, ba_ref, bb_ref, u_ref):
    u_ref[...] = _conv_glu(x_ref[...], g_ref, wa_ref, wb_ref, ba_ref, bb_ref)


def _conv_a(x, w, layer, j):
    rows = x.shape[0]
    tile = min(ROW_TILE, rows)
    return pl.pallas_call(
        _conv_a_body,
        grid=(rows // tile,),
        in_specs=[_rows(tile, D_MODEL), _vec(layer), _mat(j, D_MODEL, D_MODEL, 0), _mat(j, D_MODEL, D_MODEL, 1),
                  _vec(j, part=0), _vec(j, part=1)],
        out_specs=_rows(tile, D_MODEL),
        out_shape=jax.ShapeDtypeStruct((rows, D_MODEL), f32),
        compiler_params=_params("arbitrary"),
        name="conv_glu",
    )(x, w["norm_mix"], w["conv_w_pw1"], w["conv_w_pw1"], w["conv_b_pw1"], w["conv_b_pw1"])


def _conv_prompt_body(x_ref, g_ref, wa_ref, wb_ref, ba_ref, bb_ref, st_ref, w_ref, b_ref,
                      lg_ref, lb_ref, w2_ref, b2_ref, *rest, layer, first):
    o_ref, ns_ref, hist, cout = rest[-4:]
    t = pl.program_id(1)
    tile = x_ref.shape[0]

    @pl.when(t == 0)
    def _():
        for l in range(N_SLABS):
            hist[l, CONV_HALO - CONV_STATE:CONV_HALO, :] = st_ref[0, :, l * LANES:(l + 1) * LANES]

    @pl.when(t > 0)
    def _():
        for l in range(N_SLABS):
            hist[l, 0:CONV_HALO, :] = hist[l, tile:tile + CONV_HALO, :]

    x = x_ref[...]
    u = _conv_glu(x, g_ref, wa_ref, wb_ref, ba_ref, bb_ref)
    first_tap = CONV_HALO - CONV_STATE
    for l in range(N_SLABS):
        lanes = slice(l * LANES, (l + 1) * LANES)
        hist[l, CONV_HALO:CONV_HALO + tile, :] = u[:, lanes]
        taps = [jnp.broadcast_to(w_ref[j:j + 1, lanes], (SUBLANES, LANES)) for j in range(CONV_WIDTH)]

        def block(blk, carry, l=l, taps=taps):
            base = blk * CONV_BLOCK
            acc = [[None] * CONV_CHAINS for _ in range(CONV_STRIDE)]
            for shift in range(first_tap, first_tap + CONV_WIDTH + CONV_STRIDE - 1):
                rows = hist[l, pl.ds(base + shift, SUBLANES, stride=CONV_STRIDE), :]
                for t0 in range(CONV_STRIDE):
                    j = shift - t0 - first_tap
                    if 0 <= j < CONV_WIDTH:
                        term = rows * taps[j]
                        chain = j % CONV_CHAINS
                        acc[t0][chain] = term if acc[t0][chain] is None else acc[t0][chain] + term
            for t0 in range(CONV_STRIDE):
                cout[l, pl.ds(base + t0, SUBLANES, stride=CONV_STRIDE), :] = functools.reduce(jnp.add, acc[t0])
            return carry

        lax.fori_loop(0, tile // CONV_BLOCK, block, 0)

    c = jnp.concatenate([cout[l] for l in range(N_SLABS)], axis=1) + b_ref[...]
    o_ref[...] = _conv_tail(c, x, lg_ref, lb_ref, w2_ref, b2_ref)
    _own_layer(ns_ref, layer, first)[0] = jnp.concatenate(
        [hist[l, tile + CONV_HALO - CONV_STATE:tile + CONV_HALO, :] for l in range(N_SLABS)], axis=1)


def _conv_prompt(x, states, w, layer, j, prev, batch, seq):
    tile = CONV_TILE
    nt = seq // tile
    n_layers = states.shape[0]
    extra_specs, extra_args, aliases = _stacked(prev, 13, 1)
    seq_tile = pl.BlockSpec((tile, D_MODEL), lambda i, t: (i * nt + t, 0))
    state_block = pl.BlockSpec((None, 1, CONV_STATE, D_MODEL), lambda i, t: (j, i, 0, 0))
    return pl.pallas_call(
        functools.partial(_conv_prompt_body, layer=j, first=prev is None),
        grid=(batch, nt),
        in_specs=[seq_tile, _vec(layer), _mat(j, D_MODEL, D_MODEL, 0), _mat(j, D_MODEL, D_MODEL, 1),
                  _vec(j, part=0), _vec(j, part=1), state_block, _mat(j, CONV_WIDTH, D_MODEL), _vec(j),
                  _vec(j), _vec(j), _mat(j, D_MODEL, D_MODEL), _vec(j)] + extra_specs,
        out_specs=[seq_tile, _layer_block(prev is None, n_layers, j, (1, CONV_STATE, D_MODEL),
                                          lambda i, t: (i, 0, 0))],
        out_shape=[jax.ShapeDtypeStruct((batch * seq, D_MODEL), f32),
                   jax.ShapeDtypeStruct((n_layers, batch, CONV_STATE, D_MODEL), f32)],
        scratch_shapes=[pltpu.VMEM((N_SLABS, CONV_HALO + tile, LANES), f32),
                        pltpu.VMEM((N_SLABS, tile, LANES), f32)],
        input_output_aliases=aliases,
        compiler_params=_params("arbitrary", "arbitrary"),
        name="conv_prompt",
    )(x, w["norm_mix"], w["conv_w_pw1"], w["conv_w_pw1"], w["conv_b_pw1"], w["conv_b_pw1"], states,
      w["conv_w_dw"], w["conv_b_dw"], w["conv_ln_g"], w["conv_ln_b"], w["conv_w_pw2"], w["conv_b_pw2"],
      *extra_args)


def _conv_b_sample_body(u_ref, st_ref, w_ref, b_ref, *rest, layer, first):
    c_ref, ns_ref = rest[-2:]
    ns = _own_layer(ns_ref, layer, first)
    steps = u_ref.shape[0]
    acc = [None] * steps
    for i in range(CONV_STATE + steps):
        row = st_ref[i] if i < CONV_STATE else u_ref[i - CONV_STATE]
        for t in range(steps):
            j = i - t
            if 0 <= j < CONV_WIDTH and i <= CONV_STATE + t:
                term = row * w_ref[j:j + 1, :]
                acc[t] = term if acc[t] is None else acc[t] + term
        if i >= steps:
            ns[i - steps] = row
    for t in range(steps):
        c_ref[t] = acc[t] + b_ref[...]


def _conv_b_sample(u3, states, w, j, prev):
    steps, batch, _ = u3.shape
    bb = CONV_SAMPLE_BATCH
    n_layers = states.shape[0]
    first = prev is None
    extra_specs, extra_args, aliases = _stacked(prev, 4, 1)
    step_block = pl.BlockSpec((steps, bb, D_MODEL), lambda i: (0, i, 0))
    return pl.pallas_call(
        functools.partial(_conv_b_sample_body, layer=j, first=first),
        grid=(batch // bb,),
        in_specs=[step_block, pl.BlockSpec((None, CONV_STATE, bb, D_MODEL), lambda i: (j, 0, i, 0)),
                  _mat(j, CONV_WIDTH, D_MODEL), _vec(j)] + extra_specs,
        out_specs=[step_block,
                   _layer_block(first, n_layers, j, (CONV_STATE, bb, D_MODEL), lambda i: (0, i, 0))],
        out_shape=[jax.ShapeDtypeStruct((steps, batch, D_MODEL), f32),
                   jax.ShapeDtypeStruct((n_layers, CONV_STATE, batch, D_MODEL), f32)],
        input_output_aliases=aliases,
        compiler_params=_params("arbitrary"),
        name="conv_dw_sample",
    )(u3, states, w["conv_w_dw"], w["conv_b_dw"], *extra_args)


def _conv_tail(c, x, lg_ref, lb_ref, w2_ref, b2_ref):
    d = c - jnp.mean(c, axis=-1, keepdims=True)
    var = jnp.mean(d * d, axis=-1, keepdims=True)
    y = _silu(d * lax.rsqrt(var + EPS) * lg_ref[...] + lb_ref[...])
    return x + _dot(y.astype(bf16), w2_ref[...]) + b2_ref[...]


def _conv_c_body(c_ref, x_ref, lg_ref, lb_ref, w2_ref, b2_ref, o_ref):
    o_ref[...] = _conv_tail(c_ref[...], x_ref[...], lg_ref, lb_ref, w2_ref, b2_ref)


def _conv_c(c, x, w, j):
    rows = x.shape[0]
    tile = min(ROW_TILE, rows)
    return pl.pallas_call(
        _conv_c_body,
        grid=(rows // tile,),
        in_specs=[_rows(tile, D_MODEL), _rows(tile, D_MODEL), _vec(j), _vec(j), _mat(j, D_MODEL, D_MODEL), _vec(j)],
        out_specs=_rows(tile, D_MODEL),
        out_shape=jax.ShapeDtypeStruct((rows, D_MODEL), f32),
        compiler_params=_params("arbitrary"),
        name="conv_out",
    )(c, x, w["conv_ln_g"], w["conv_ln_b"], w["conv_w_pw2"], w["conv_b_pw2"])


def _gla_a_body(x_ref, g_ref, wq_ref, wk_ref, wv_ref, wr_ref, wz_ref, wg2_ref, bg_ref,
                q_ref, k_ref, v_ref, r_ref, la_ref):
    q, k, v, r, la = _gla_project(x_ref[...], g_ref, wq_ref, wk_ref, wv_ref, wr_ref, wz_ref, wg2_ref, bg_ref)
    q_ref[...] = q
    k_ref[...] = k
    v_ref[...] = v
    r_ref[...] = r
    la_ref[...] = la


def _gla_project(x, g_ref, wq_ref, wk_ref, wv_ref, wr_ref, wz_ref, wg2_ref, bg_ref):
    h = _rms(x, g_ref[...]).astype(bf16)
    q = _dot(h, wq_ref[...]) * (GLA_DK ** -0.5)
    k = _dot(h, wk_ref[...])
    v = _dot(h, wv_ref[...])
    r = _dot(h, wr_ref[...])
    z = _dot(h, wz_ref[...])
    pre = _dot(z.astype(bf16), wg2_ref[...]) + bg_ref[...]
    la = (jnp.minimum(pre, 0.0) - jnp.log1p(jnp.exp(-jnp.abs(pre)))) * (1.0 / GLA_TAU)
    return q, k, v, r, la


def _gla_a(x, w, layer, j):
    rows = x.shape[0]
    tile = min(ROW_TILE, rows)
    widths = (GLA_DKT, GLA_DKT, GLA_DVT, GLA_DVT, GLA_DKT)
    return pl.pallas_call(
        _gla_a_body,
        grid=(rows // tile,),
        in_specs=[_rows(tile, D_MODEL), _vec(layer),
                  _mat(j, D_MODEL, GLA_DKT, 0), _mat(j, D_MODEL, GLA_DKT, 1),
                  _mat(j, D_MODEL, GLA_DVT, 1), _mat(j, D_MODEL, GLA_DVT, 2),
                  _mat(j, D_MODEL, LANES), _mat(j, LANES, GLA_DKT), _vec(j, GLA_DKT)],
        out_specs=[_rows(tile, n) for n in widths],
        out_shape=[jax.ShapeDtypeStruct((rows, n), f32) for n in widths],
        compiler_params=_params("arbitrary"),
        name="gla_proj",
    )(x, w["norm_mix"], w["gla_w_in"], w["gla_w_in"], w["gla_w_in"], w["gla_w_in"], w["gla_w_z"],
      w["gla_w_gate2"], w["gla_b_gate"])


def _split_bf16(x):
    hi = x.astype(bf16)
    return hi, (x - hi.astype(f32)).astype(bf16)


def _decay_columns(la_hi, la_lo, ones):
    total = _dot_t_lhs(la_hi, ones) + _dot_t_lhs(la_lo, ones)
    return jnp.concatenate([jnp.exp(total)] * (GLA_DV // LANES), axis=1)


def _gla_prompt_body(x_ref, g_ref, wq_ref, wk_ref, wv_ref, wr_ref, wz_ref, wg2_ref, bg_ref, s0_ref,
                     go_ref, wo_ref, *rest, layer, first):
    out_ref, so_ref, state, scores = rest[-4:]
    c = pl.program_id(1)
    seqs, chunk, _ = x_ref.shape

    @pl.when(c == 0)
    def _():
        state[...] = s0_ref[...]

    x = x_ref[...].reshape(seqs * chunk, D_MODEL)
    q, k, v, r, la = _gla_project(x, g_ref, wq_ref, wk_ref, wv_ref, wr_ref, wz_ref, wg2_ref, bg_ref)
    la_hi, la_lo = _split_bf16(la)
    tri = (lax.broadcasted_iota(jnp.int32, (chunk, chunk), 1)
           <= lax.broadcasted_iota(jnp.int32, (chunk, chunk), 0)).astype(bf16)
    ones = jnp.ones((chunk, LANES), bf16)
    gated = [_gla_chunk(slice(s * chunk, (s + 1) * chunk), q, k, v, r, la_hi, la_lo, tri, ones,
                        state.at[s], scores.at[s], go_ref) for s in range(seqs)]
    y = x + _dot(jnp.concatenate(gated, axis=0), wo_ref[...])
    out_ref[...] = y.reshape(seqs, chunk, D_MODEL)

    @pl.when(c == pl.num_programs(1) - 1)
    def _():
        _own_layer(so_ref, layer, first)[...] = state[...]


def _gla_chunk(rows, q, k, v, r, la_hi, la_lo, tri, ones, state, scores, go_ref):
    chunk = tri.shape[0]
    la_hi, la_lo = la_hi[rows], la_lo[rows]
    b_all = _dot(tri, la_hi) + _dot(tri, la_lo)
    gated = []
    for h in range(GLA_HEADS):
        ks = slice(h * GLA_DK, (h + 1) * GLA_DK)
        vs = slice(h * GLA_DV, (h + 1) * GLA_DV)
        b = b_all[:, ks]
        qh = q[rows, ks]
        kh = k[rows, ks]
        vb = v[rows, vs].astype(bf16)
        b_last = b[chunk - 1:chunk, :]
        q_dec = (qh * jnp.exp(b)).astype(bf16)
        k_end = (kh * jnp.exp(b_last - b)).astype(bf16)
        s_old = state[h]
        o_inter = _dot(q_dec, s_old.astype(bf16))

        for i in range(chunk // GLA_SUB):
            r0 = i * GLA_SUB
            nk = r0 + GLA_SUB
            npad = LANES * (-(-nk // LANES))
            bq = b[r0:nk]
            bk = b[0:nk]
            if i == 0:
                q_exp, k_exp = bq, -bk
            else:
                ref = b[r0 - 1:r0, :]
                q_exp, k_exp = bq - ref, ref - bk
            qi = (qh[r0:nk] * jnp.exp(q_exp)).astype(bf16)
            ki = (kh[0:nk] * jnp.exp(k_exp)).astype(bf16)
            if nk < npad:
                ki = jnp.concatenate([ki, jnp.zeros((npad - nk, GLA_DK), bf16)], axis=0)
            s = _dot_t_rhs(qi, ki)
            causal = (lax.broadcasted_iota(jnp.int32, (GLA_SUB, npad), 1)
                      <= lax.broadcasted_iota(jnp.int32, (GLA_SUB, npad), 0) + r0)
            scores[h, r0:nk, 0:npad] = jnp.where(causal, s, 0.0).astype(bf16)
            if npad < chunk:
                scores[h, r0:nk, npad:chunk] = jnp.zeros((GLA_SUB, chunk - npad), bf16)

        o = o_inter + _dot(scores[h], vb)
        gated.append(_gla_gate(o, r[rows, vs], go_ref))
        state[h] = s_old * _decay_columns(la_hi[:, ks], la_lo[:, ks], ones) + _dot_t_lhs(k_end, vb)
    return jnp.concatenate(gated, axis=1)


def _gla_prompt(x, states, w, layer, j, prev, batch, seq):
    chunk, seqs = GLA_CHUNK, GLA_SEQS
    n_layers = states.shape[0]
    first = prev is None
    extra_specs, extra_args, aliases = _stacked(prev, 12, 1)
    seq_tile = pl.BlockSpec((seqs, chunk, D_MODEL), lambda i, c: (i, c, 0))
    state_shape = (seqs, GLA_HEADS, GLA_DK, GLA_DV)
    out, new_states = pl.pallas_call(
        functools.partial(_gla_prompt_body, layer=j, first=first),
        grid=(batch // seqs, seq // chunk),
        in_specs=[seq_tile, _vec(layer),
                  _mat(j, D_MODEL, GLA_DKT, 0), _mat(j, D_MODEL, GLA_DKT, 1),
                  _mat(j, D_MODEL, GLA_DVT, 1), _mat(j, D_MODEL, GLA_DVT, 2),
                  _mat(j, D_MODEL, LANES), _mat(j, LANES, GLA_DKT), _vec(j, GLA_DKT),
                  pl.BlockSpec((None,) + state_shape, lambda i, c: (j, i, 0, 0, 0)),
                  _vec(j, GLA_DV), _mat(j, GLA_DVT, D_MODEL)] + extra_specs,
        out_specs=[seq_tile, _layer_block(first, n_layers, j, state_shape, lambda i, c: (i, 0, 0, 0))],
        out_shape=[jax.ShapeDtypeStruct((batch, seq, D_MODEL), f32),
                   jax.ShapeDtypeStruct((n_layers, batch, GLA_HEADS, GLA_DK, GLA_DV), f32)],
        scratch_shapes=[pltpu.VMEM(state_shape, f32), pltpu.VMEM((seqs, GLA_HEADS, chunk, chunk), bf16)],
        input_output_aliases=aliases,
        compiler_params=_params("arbitrary", "arbitrary"),
        name="gla_prompt",
    )(x.reshape(batch, seq, D_MODEL), w["norm_mix"], w["gla_w_in"], w["gla_w_in"], w["gla_w_in"], w["gla_w_in"],
      w["gla_w_z"], w["gla_w_gate2"], w["gla_b_gate"], states, w["gla_onorm_g"], w["gla_w_out"], *extra_args)
    return out.reshape(batch * seq, D_MODEL), new_states


def _gla_b_sample_body(q_ref, k_ref, v_ref, la_ref, s0_ref, *rest, layer, first):
    o_ref, so_ref = rest[-2:]
    so = _own_layer(so_ref, layer, first)
    steps, bb, _ = q_ref.shape
    causal = (lax.broadcasted_iota(jnp.int32, (steps, steps), 1)
              <= lax.broadcasted_iota(jnp.int32, (steps, steps), 0))
    ones = jnp.ones((steps, LANES), bf16)

    for i in range(bb):
        la = la_ref[:, i, :]
        q, k, v = q_ref[:, i, :], k_ref[:, i, :], v_ref[:, i, :]
        cum = [la[0:1]]
        for t in range(1, steps):
            cum.append(cum[-1] + la[t:t + 1])
        b_all = jnp.concatenate(cum, axis=0)
        la_hi, la_lo = _split_bf16(la)
        heads = []
        for h in range(GLA_HEADS):
            ks = slice(h * GLA_DK, (h + 1) * GLA_DK)
            vs = slice(h * GLA_DV, (h + 1) * GLA_DV)
            b = b_all[:, ks]
            qh = q[:, ks]
            kh = k[:, ks]
            vb = v[:, vs].astype(bf16)
            b_last = b[steps - 1:steps, :]
            q_dec = (qh * jnp.exp(b)).astype(bf16)
            k_inv = (kh * jnp.exp(-b)).astype(bf16)
            k_end = (kh * jnp.exp(b_last - b)).astype(bf16)
            s_old = s0_ref[i, h]
            sc = jnp.where(causal, _dot_t_rhs(q_dec, k_inv), 0.0).astype(bf16)
            heads.append(_dot(sc, vb) + _dot(q_dec, s_old.astype(bf16)))
            so[i, h] = s_old * _decay_columns(la_hi[:, ks], la_lo[:, ks], ones) + _dot_t_lhs(k_end, vb)
        o_ref[:, i, :] = jnp.concatenate(heads, axis=1)


def _gla_b_sample(q, k, v, la, states, j, prev):
    steps, batch, _ = q.shape
    bb = GLA_SAMPLE_BATCH
    n_layers = states.shape[0]
    first = prev is None
    extra_specs, extra_args, aliases = _stacked(prev, 5, 1)
    step_block = lambda n: pl.BlockSpec((steps, bb, n), lambda i: (0, i, 0))
    state_shape = (bb, GLA_HEADS, GLA_DK, GLA_DV)
    return pl.pallas_call(
        functools.partial(_gla_b_sample_body, layer=j, first=first),
        grid=(batch // bb,),
        in_specs=[step_block(GLA_DKT), step_block(GLA_DKT), step_block(GLA_DVT), step_block(GLA_DKT),
                  pl.BlockSpec((None,) + state_shape, lambda i: (j, i, 0, 0, 0))] + extra_specs,
        out_specs=[step_block(GLA_DVT),
                   _layer_block(first, n_layers, j, state_shape, lambda i: (i, 0, 0, 0))],
        out_shape=[jax.ShapeDtypeStruct((steps, batch, GLA_DVT), f32),
                   jax.ShapeDtypeStruct((n_layers, batch, GLA_HEADS, GLA_DK, GLA_DV), f32)],
        input_output_aliases=aliases,
        compiler_params=_params("arbitrary"),
        name="gla_core_sample",
    )(q, k, v, la, states, *extra_args)


def _gla_gate(o, r, go_ref):
    return (_rms(o, go_ref[...]) * _silu(r)).astype(bf16)


def _gla_c_body(o_ref, r_ref, x_ref, go_ref, wo_ref, out_ref):
    parts = []
    for h in range(GLA_HEADS):
        vs = slice(h * GLA_DV, (h + 1) * GLA_DV)
        parts.append(_gla_gate(o_ref[:, vs], r_ref[:, vs], go_ref))
    out_ref[...] = x_ref[...] + _dot(jnp.concatenate(parts, axis=1), wo_ref[...])


def _gla_c(o, r, x, w, j):
    rows = x.shape[0]
    tile = min(ROW_TILE, rows)
    return pl.pallas_call(
        _gla_c_body,
        grid=(rows // tile,),
        in_specs=[_rows(tile, GLA_DVT), _rows(tile, GLA_DVT), _rows(tile, D_MODEL), _vec(j, GLA_DV),
                  _mat(j, GLA_DVT, D_MODEL)],
        out_specs=_rows(tile, D_MODEL),
        out_shape=jax.ShapeDtypeStruct((rows, D_MODEL), f32),
        compiler_params=_params("arbitrary"),
        name="gla_out",
    )(o, r, x, w["gla_onorm_g"], w["gla_w_out"])


def _trunk(x, conv_state, gla_state, w, batch, seq, *, prompt):
    new_conv, new_gla = None, None
    for i in range(DEPTH):
        j = i // 2
        x = _ffn(x, w, "norm_ffn_pre", i, 0)
        if i % 2 == 0 and prompt:
            x, new_conv = _conv_prompt(x, conv_state, w, i, j, new_conv, batch, seq)
        elif i % 2 == 0:
            u = _conv_a(x, w, i, j)
            c, new_conv = _conv_b_sample(u.reshape(seq, batch, D_MODEL), conv_state, w, j, new_conv)
            x = _conv_c(c.reshape(seq * batch, D_MODEL), x, w, j)
        elif prompt:
            x, new_gla = _gla_prompt(x, gla_state, w, i, j, new_gla, batch, seq)
        else:
            q, k, v, r, la = _gla_a(x, w, i, j)
            steps = lambda a: a.reshape(seq, batch, a.shape[-1])
            o, new_gla = _gla_b_sample(steps(q), steps(k), steps(v), steps(la), gla_state, j, new_gla)
            x = _gla_c(o.reshape(seq * batch, GLA_DVT), r, x, w, j)
        x = _ffn(x, w, "norm_ffn_post", i, 1, final_norm=(i == DEPTH - 1))
    return x, new_conv, new_gla


def kernel(x_prompt, x_sample, state_conv, state_gla, norm_ffn_pre, norm_mix, norm_ffn_post, norm_final,
           ffn_w_gate, ffn_w_up, ffn_w_down, conv_w_pw1, conv_b_pw1, conv_w_dw, conv_b_dw, conv_ln_g, conv_ln_b,
           conv_w_pw2, conv_b_pw2, gla_w_in, gla_w_gate2, gla_b_gate, gla_onorm_g, gla_w_out):
    row = lambda a: a[..., None, :]
    z_start = 2 * GLA_DKT + 2 * GLA_DVT
    w_z = jnp.pad(gla_w_in[:, :, z_start:], ((0, 0), (0, 0), (0, LANES - GLA_GATE_RANK)))
    w_gate2 = jnp.pad(gla_w_gate2, ((0, 0), (0, LANES - GLA_GATE_RANK), (0, 0)))
    w = dict(
        norm_ffn_pre=row(norm_ffn_pre), norm_mix=row(norm_mix), norm_ffn_post=row(norm_ffn_post),
        norm_final=norm_final[None, :],
        ffn_w_gate=ffn_w_gate.astype(bf16), ffn_w_up=ffn_w_up.astype(bf16), ffn_w_down=ffn_w_down.astype(bf16),
        conv_w_pw1=conv_w_pw1.astype(bf16), conv_b_pw1=row(conv_b_pw1),
        conv_w_dw=conv_w_dw, conv_b_dw=row(conv_b_dw), conv_ln_g=row(conv_ln_g), conv_ln_b=row(conv_ln_b),
        conv_w_pw2=conv_w_pw2.astype(bf16), conv_b_pw2=row(conv_b_pw2),
        gla_w_in=gla_w_in.astype(bf16), gla_w_z=w_z.astype(bf16), gla_w_gate2=w_gate2.astype(bf16),
        gla_b_gate=row(gla_b_gate), gla_onorm_g=row(gla_onorm_g), gla_w_out=gla_w_out.astype(bf16),
    )
    batch, seq, _ = x_prompt.shape
    conv0 = jnp.zeros((state_conv.shape[0], batch) + state_conv.shape[2:], x_prompt.dtype)
    gla0 = jnp.zeros((state_gla.shape[0], batch) + state_gla.shape[2:], x_prompt.dtype)
    y_p, conv_p, gla_p = _trunk(x_prompt.reshape(batch * seq, D_MODEL), conv0, gla0, w, batch, seq, prompt=True)
    y_p = y_p.reshape(batch, seq, D_MODEL)

    dec_batch, steps, _ = x_sample.shape
    x_s = x_sample.transpose(1, 0, 2).reshape(steps * dec_batch, D_MODEL)
    y_s, conv_s, gla_s = _trunk(x_s, state_conv.transpose(0, 2, 1, 3), state_gla, w, dec_batch, steps,
                                prompt=False)
    y_s = y_s.reshape(steps, dec_batch, D_MODEL).transpose(1, 0, 2)
    return (y_p, y_s, conv_p, gla_p, conv_s.transpose(0, 2, 1, 3), gla_s)
```

```python
import functools

import jax
import jax.numpy as jnp
from jax import lax
from jax.experimental import pallas as pl
from jax.experimental.pallas import tpu as pltpu

f32 = jnp.float32
bf16 = jnp.bfloat16

D_MODEL = 1024
D_FF = 2816
DEPTH = 4
CONV_WIDTH = 31
CONV_STATE = CONV_WIDTH - 1
GLA_HEADS = 4
GLA_DK = 128
GLA_DV = 256
GLA_DKT = GLA_HEADS * GLA_DK
GLA_DVT = GLA_HEADS * GLA_DV
GLA_GATE_RANK = 16
GLA_TAU = 16.0
EPS = 1e-6

LANES = 128
SUBLANES = 8
N_SLABS = D_MODEL // LANES
VMEM_LIMIT = 56 * 1024 * 1024

ROW_TILE = 512
FFN_TILE = 1024
FFN_CHUNK = 256
CONV_TILE = 512
CONV_HALO = 32
CONV_STRIDE = 4
CONV_BLOCK = SUBLANES * CONV_STRIDE
CONV_CHAINS = 3
CONV_SAMPLE_BATCH = 8
GLA_CHUNK = 256
GLA_SEQS = 2
GLA_SUB = 32
GLA_SAMPLE_BATCH = 8


def _params(*semantics):
    return pltpu.CompilerParams(dimension_semantics=semantics, vmem_limit_bytes=VMEM_LIMIT)


def _resident(block, index=None):
    index = (0,) * len(block) if index is None else index
    return pl.BlockSpec(block, lambda *_: index, pipeline_mode=pl.Buffered(1))


def _vec(layer, width=D_MODEL, part=0):
    return _resident((None, 1, width), (layer, 0, part))


def _mat(layer, rows, cols, part=0):
    return _resident((None, rows, cols), (layer, 0, part))


def _rows(tile, width):
    return pl.BlockSpec((tile, width), lambda i: (i, 0))


def _stacked(prev, in_count, out_index):
    if prev is None:
        return [], [], {}
    return [pl.BlockSpec(memory_space=pl.ANY)], [prev], {in_count: out_index}


def _layer_block(first, n_layers, layer, block, index):
    lead, at = (n_layers, 0) if first else (None, layer)
    return pl.BlockSpec((lead,) + block, lambda *grid: (at,) + index(*grid))


def _own_layer(ref, layer, first):
    if not first:
        return ref
    for other in range(ref.shape[0]):
        if other != layer:
            ref[other] = jnp.zeros(ref.shape[1:], ref.dtype)
    return ref.at[layer]


def _rms(x, g):
    return x * lax.rsqrt(jnp.mean(x * x, axis=-1, keepdims=True) + EPS) * g


def _silu(x):
    return x * jax.nn.sigmoid(x)


def _dot(a, b):
    return jnp.dot(a, b, preferred_element_type=f32)


def _dot_t_rhs(a, b):
    return lax.dot_general(a, b, (((1,), (1,)), ((), ())), preferred_element_type=f32)


def _dot_t_lhs(a, b):
    return lax.dot_general(a, b, (((0,), (0,)), ((), ())), preferred_element_type=f32)


def _ffn_rows(x, g_ref, wg_ref, wu_ref, wd_ref, gf_ref, a_scr, final_norm):
    rows = x.shape[0]
    h = _rms(x, g_ref[...]).astype(bf16)
    for j in range(D_FF // FFN_CHUNK):
        cols = pl.ds(j * FFN_CHUNK, FFN_CHUNK)
        gate = _dot(h, wg_ref[:, cols])
        up = _dot(h, wu_ref[:, cols])
        a_scr[0:rows, cols] = (_silu(gate) * up).astype(bf16)
    y = x + 0.5 * _dot(a_scr[0:rows, :], wd_ref[...])
    return _rms(y, gf_ref[...]) if final_norm else y


def _ffn_body(xp_ref, xs_ref, g_ref, wg_ref, wu_ref, wd_ref, gf_ref, *rest, final_norm, cast_next):
    op_ref, os_ref = rest[3:5] if cast_next else rest[0:2]
    a_scr = rest[-1]
    weights = (g_ref, wg_ref, wu_ref, wd_ref, gf_ref, a_scr, final_norm)
    op_ref[...] = _ffn_rows(xp_ref[...], *weights)

    @pl.when(pl.program_id(0) == pl.num_programs(0) - 1)
    def _():
        os_ref[...] = _ffn_rows(xs_ref[...], *weights)

    if cast_next:
        for src, dst in zip(rest[0:3], rest[5:8]):
            dst[...] = src[...].astype(bf16)


def _ffn(xp, xs, wb, w, gain, layer, nxt, *, final_norm=False):
    rows = xp.shape[0]
    tile = FFN_TILE
    steps = rows // tile
    cast_next = nxt is not None
    in_specs = [_rows(tile, D_MODEL), _resident(xs.shape), _vec(layer), _resident((D_MODEL, D_FF)),
                _resident((D_MODEL, D_FF)), _resident((D_FF, D_MODEL)), _resident((1, D_MODEL))]
    out_specs = [_rows(tile, D_MODEL), pl.BlockSpec(xs.shape, lambda i: (0, 0))]
    out_shape = [jax.ShapeDtypeStruct(xp.shape, f32), jax.ShapeDtypeStruct(xs.shape, f32)]
    args = [xp, xs, w[gain], *wb, w["norm_final"]]
    if cast_next:
        for name, (r, c) in (("ffn_w_gate", (D_MODEL, D_FF)), ("ffn_w_up", (D_MODEL, D_FF)),
                             ("ffn_w_down", (D_FF, D_MODEL))):
            in_specs.append(pl.BlockSpec((None, None, r // steps, c), lambda i: nxt + (i, 0)))
            out_specs.append(pl.BlockSpec((r // steps, c), lambda i: (i, 0)))
            out_shape.append(jax.ShapeDtypeStruct((r, c), bf16))
            args.append(w[name])
    outs = pl.pallas_call(
        functools.partial(_ffn_body, final_norm=final_norm, cast_next=cast_next),
        grid=(steps,),
        in_specs=in_specs,
        out_specs=out_specs,
        out_shape=out_shape,
        scratch_shapes=[pltpu.VMEM((tile, D_FF), bf16)],
        compiler_params=_params("arbitrary"),
        name="ffn",
    )(*args)
    return outs[0], outs[1], (tuple(outs[2:]) if cast_next else None)


def _conv_glu(x, g_ref, wa_ref, wb_ref, ba_ref, bb_ref):
    h = _rms(x, g_ref[...]).astype(bf16)
    a = _dot(h, wa_ref[...]) + ba_ref[...]
    gate = _dot(h, wb_ref[...]) + bb_ref[...]
    return a * jax.nn.sigmoid(gate)


def _conv_a_body(x_ref, g_ref, wa_ref, wb_ref, ba_ref, bb_ref, u_ref):
    u_ref[...] = _conv_glu(x_ref[...], g_ref, wa_ref, wb_ref, ba_ref, bb_ref)


def _conv_a(x, w, layer, j):
    rows = x.shape[0]
    tile = min(ROW_TILE, rows)
    return pl.pallas_call(
        _conv_a_body,
        grid=(rows // tile,),
        in_specs=[_rows(tile, D_MODEL), _vec(layer), _mat(j, D_MODEL, D_MODEL, 0), _mat(j, D_MODEL, D_MODEL, 1),
                  _vec(j, part=0), _vec(j, part=1)],
        out_specs=_rows(tile, D_MODEL),
        out_shape=jax.ShapeDtypeStruct((rows, D_MODEL), f32),
        compiler_params=_params("arbitrary"),
        name="conv_glu",
    )(x, w["norm_mix"], w["conv_w_pw1"], w["conv_w_pw1"], w["conv_b_pw1"], w["conv_b_pw1"])


def _conv_prompt_body(x_ref, g_ref, wa_ref, wb_ref, ba_ref, bb_ref, st_ref, w_ref, b_ref,
                      lg_ref, lb_ref, w2_ref, b2_ref, *rest, layer, first):
    o_ref, ns_ref, hist, cout = rest[-4:]
    t = pl.program_id(1)
    tile = x_ref.shape[0]

    @pl.when(t == 0)
    def _():
        for l in range(N_SLABS):
            hist[l, CONV_HALO - CONV_STATE:CONV_HALO, :] = st_ref[0, :, l * LANES:(l + 1) * LANES]

    @pl.when(t > 0)
    def _():
        for l in range(N_SLABS):
            hist[l, 0:CONV_HALO, :] = hist[l, tile:tile + CONV_HALO, :]

    x = x_ref[...]
    u = _conv_glu(x, g_ref, wa_ref, wb_ref, ba_ref, bb_ref)
    first_tap = CONV_HALO - CONV_STATE
    for l in range(N_SLABS):
        lanes = slice(l * LANES, (l + 1) * LANES)
        hist[l, CONV_HALO:CONV_HALO + tile, :] = u[:, lanes]
        taps = [jnp.broadcast_to(w_ref[j:j + 1, lanes], (SUBLANES, LANES)) for j in range(CONV_WIDTH)]

        def block(blk, carry, l=l, taps=taps):
            base = blk * CONV_BLOCK
            acc = [[None] * CONV_CHAINS for _ in range(CONV_STRIDE)]
            for shift in range(first_tap, first_tap + CONV_WIDTH + CONV_STRIDE - 1):
                rows = hist[l, pl.ds(base + shift, SUBLANES, stride=CONV_STRIDE), :]
                for t0 in range(CONV_STRIDE):
                    j = shift - t0 - first_tap
                    if 0 <= j < CONV_WIDTH:
                        term = rows * taps[j]
                        chain = j % CONV_CHAINS
                        acc[t0][chain] = term if acc[t0][chain] is None else acc[t0][chain] + term
            for t0 in range(CONV_STRIDE):
                cout[l, pl.ds(base + t0, SUBLANES, stride=CONV_STRIDE), :] = functools.reduce(jnp.add, acc[t0])
            return carry

        lax.fori_loop(0, tile // CONV_BLOCK, block, 0)

    c = jnp.concatenate([cout[l] for l in range(N_SLABS)], axis=1) + b_ref[...]
    o_ref[...] = _conv_tail(c, x, lg_ref, lb_ref, w2_ref, b2_ref)
    _own_layer(ns_ref, layer, first)[0] = jnp.concatenate(
        [hist[l, tile + CONV_HALO - CONV_STATE:tile + CONV_HALO, :] for l in range(N_SLABS)], axis=1)


def _conv_prompt(x, states, w, layer, j, prev, batch, seq):
    tile = CONV_TILE
    nt = seq // tile
    n_layers = states.shape[0]
    extra_specs, extra_args, aliases = _stacked(prev, 13, 1)
    seq_tile = pl.BlockSpec((tile, D_MODEL), lambda i, t: (i * nt + t, 0))
    state_block = pl.BlockSpec((None, 1, CONV_STATE, D_MODEL), lambda i, t: (j, i, 0, 0))
    return pl.pallas_call(
        functools.partial(_conv_prompt_body, layer=j, first=prev is None),
        grid=(batch, nt),
        in_specs=[seq_tile, _vec(layer), _mat(j, D_MODEL, D_MODEL, 0), _mat(j, D_MODEL, D_MODEL, 1),
                  _vec(j, part=0), _vec(j, part=1), state_block, _mat(j, CONV_WIDTH, D_MODEL), _vec(j),
                  _vec(j), _vec(j), _mat(j, D_MODEL, D_MODEL), _vec(j)] + extra_specs,
        out_specs=[seq_tile, _layer_block(prev is None, n_layers, j, (1, CONV_STATE, D_MODEL),
                                          lambda i, t: (i, 0, 0))],
        out_shape=[jax.ShapeDtypeStruct((batch * seq, D_MODEL), f32),
                   jax.ShapeDtypeStruct((n_layers, batch, CONV_STATE, D_MODEL), f32)],
        scratch_shapes=[pltpu.VMEM((N_SLABS, CONV_HALO + tile, LANES), f32),
                        pltpu.VMEM((N_SLABS, tile, LANES), f32)],
        input_output_aliases=aliases,
        compiler_params=_params("arbitrary", "arbitrary"),
        name="conv_prompt",
    )(x, w["norm_mix"], w["conv_w_pw1"], w["conv_w_pw1"], w["conv_b_pw1"], w["conv_b_pw1"], states,
      w["conv_w_dw"], w["conv_b_dw"], w["conv_ln_g"], w["conv_ln_b"], w["conv_w_pw2"], w["conv_b_pw2"],
      *extra_args)


def _conv_b_sample_body(u_ref, st_ref, w_ref, b_ref, *rest, layer, first):
    c_ref, ns_ref = rest[-2:]
    ns = _own_layer(ns_ref, layer, first)
    steps = u_ref.shape[0]
    acc = [None] * steps
    for i in range(CONV_STATE + steps):
        row = st_ref[i] if i < CONV_STATE else u_ref[i - CONV_STATE]
        for t in range(steps):
            j = i - t
            if 0 <= j < CONV_WIDTH and i <= CONV_STATE + t:
                term = row * w_ref[j:j + 1, :]
                acc[t] = term if acc[t] is None else acc[t] + term
        if i >= steps:
            ns[i - steps] = row
    for t in range(steps):
        c_ref[t] = acc[t] + b_ref[...]


def _conv_b_sample(u3, states, w, j, prev):
    steps, batch, _ = u3.shape
    bb = CONV_SAMPLE_BATCH
    n_layers = states.shape[0]
    first = prev is None
    extra_specs, extra_args, aliases = _stacked(prev, 4, 1)
    step_block = pl.BlockSpec((steps, bb, D_MODEL), lambda i: (0, i, 0))
    return pl.pallas_call(
        functools.partial(_conv_b_sample_body, layer=j, first=first),
        grid=(batch // bb,),
        in_specs=[step_block, pl.BlockSpec((None, CONV_STATE, bb, D_MODEL), lambda i: (j, 0, i, 0)),
                  _mat(j, CONV_WIDTH, D_MODEL), _vec(j)] + extra_specs,
        out_specs=[step_block,
                   _layer_block(first, n_layers, j, (CONV_STATE, bb, D_MODEL), lambda i: (0, i, 0))],
        out_shape=[jax.ShapeDtypeStruct((steps, batch, D_MODEL), f32),
                   jax.ShapeDtypeStruct((n_layers, CONV_STATE, batch, D_MODEL), f32)],
        input_output_aliases=aliases,
        compiler_params=_params("arbitrary"),
        name="conv_dw_sample",
    )(u3, states, w["conv_w_dw"], w["conv_b_dw"], *extra_args)


def _conv_tail(c, x, lg_ref, lb_ref, w2_ref, b2_ref):
    d = c - jnp.mean(c, axis=-1, keepdims=True)
    var = jnp.mean(d * d, axis=-1, keepdims=True)
    y = _silu(d * lax.rsqrt(var + EPS) * lg_ref[...] + lb_ref[...])
    return x + _dot(y.astype(bf16), w2_ref[...]) + b2_ref[...]


def _conv_c_body(c_ref, x_ref, lg_ref, lb_ref, w2_ref, b2_ref, o_ref):
    o_ref[...] = _conv_tail(c_ref[...], x_ref[...], lg_ref, lb_ref, w2_ref, b2_ref)


def _conv_c(c, x, w, j):
    rows = x.shape[0]
    tile = min(ROW_TILE, rows)
    return pl.pallas_call(
        _conv_c_body,
        grid=(rows // tile,),
        in_specs=[_rows(tile, D_MODEL), _rows(tile, D_MODEL), _vec(j), _vec(j), _mat(j, D_MODEL, D_MODEL), _vec(j)],
        out_specs=_rows(tile, D_MODEL),
        out_shape=jax.ShapeDtypeStruct((rows, D_MODEL), f32),
        compiler_params=_params("arbitrary"),
        name="conv_out",
    )(c, x, w["conv_ln_g"], w["conv_ln_b"], w["conv_w_pw2"], w["conv_b_pw2"])


def _gla_a_body(x_ref, g_ref, wq_ref, wk_ref, wv_ref, wr_ref, wz_ref, wg2_ref, bg_ref,
                q_ref, k_ref, v_ref, r_ref, la_ref):
    q, k, v, r, la = _gla_project(x_ref[...], g_ref, wq_ref, wk_ref, wv_ref, wr_ref, wz_ref, wg2_ref, bg_ref)
    q_ref[...] = q
    k_ref[...] = k
    v_ref[...] = v
    r_ref[...] = r
    la_ref[...] = la


def _gla_project(x, g_ref, wq_ref, wk_ref, wv_ref, wr_ref, wz_ref, wg2_ref, bg_ref):
    h = _rms(x, g_ref[...]).astype(bf16)
    q = _dot(h, wq_ref[...]) * (GLA_DK ** -0.5)
    k = _dot(h, wk_ref[...])
    v = _dot(h, wv_ref[...])
    r = _dot(h, wr_ref[...])
    z = _dot(h, wz_ref[...])
    pre = _dot(z.astype(bf16), wg2_ref[...]) + bg_ref[...]
    la = (jnp.minimum(pre, 0.0) - jnp.log1p(jnp.exp(-jnp.abs(pre)))) * (1.0 / GLA_TAU)
    return q, k, v, r, la


def _gla_a(x, w, layer, j):
    rows = x.shape[0]
    tile = min(ROW_TILE, rows)
    widths = (GLA_DKT, GLA_DKT, GLA_DVT, GLA_DVT, GLA_DKT)
    return pl.pallas_call(
        _gla_a_body,
        grid=(rows // tile,),
        in_specs=[_rows(tile, D_MODEL), _vec(layer),
                  _mat(j, D_MODEL, GLA_DKT, 0), _mat(j, D_MODEL, GLA_DKT, 1),
                  _mat(j, D_MODEL, GLA_DVT, 1), _mat(j, D_MODEL, GLA_DVT, 2),
                  _mat(j, D_MODEL, LANES), _mat(j, LANES, GLA_DKT), _vec(j, GLA_DKT)],
        out_specs=[_rows(tile, n) for n in widths],
        out_shape=[jax.ShapeDtypeStruct((rows, n), f32) for n in widths],
        compiler_params=_params("arbitrary"),
        name="gla_proj",
    )(x, w["norm_mix"], w["gla_w_in"], w["gla_w_in"], w["gla_w_in"], w["gla_w_in"], w["gla_w_z"],
      w["gla_w_gate2"], w["gla_b_gate"])


def _split_bf16(x):
    hi = x.astype(bf16)
    return hi, (x - hi.astype(f32)).astype(bf16)


def _decay_columns(la_hi, la_lo, ones):
    total = _dot_t_lhs(la_hi, ones) + _dot_t_lhs(la_lo, ones)
    return jnp.concatenate([jnp.exp(total)] * (GLA_DV // LANES), axis=1)


def _gla_prompt_body(x_ref, g_ref, wq_ref, wk_ref, wv_ref, wr_ref, wz_ref, wg2_ref, bg_ref, s0_ref,
                     go_ref, wo_ref, *rest, layer, first):
    out_ref, so_ref, state, scores = rest[-4:]
    c = pl.program_id(1)
    seqs, chunk, _ = x_ref.shape

    @pl.when(c == 0)
    def _():
        state[...] = s0_ref[...]

    x = x_ref[...].reshape(seqs * chunk, D_MODEL)
    q, k, v, r, la = _gla_project(x, g_ref, wq_ref, wk_ref, wv_ref, wr_ref, wz_ref, wg2_ref, bg_ref)
    la_hi, la_lo = _split_bf16(la)
    tri = (lax.broadcasted_iota(jnp.int32, (chunk, chunk), 1)
           <= lax.broadcasted_iota(jnp.int32, (chunk, chunk), 0)).astype(bf16)
    ones = jnp.ones((chunk, LANES), bf16)
    gated = [_gla_chunk(slice(s * chunk, (s + 1) * chunk), q, k, v, r, la_hi, la_lo, tri, ones,
                        state.at[s], scores.at[s], go_ref) for s in range(seqs)]
    y = x + _dot(jnp.concatenate(gated, axis=0), wo_ref[...])
    out_ref[...] = y.reshape(seqs, chunk, D_MODEL)

    @pl.when(c == pl.num_programs(1) - 1)
    def _():
        _own_layer(so_ref, layer, first)[...] = state[...]


def _gla_chunk(rows, q, k, v, r, la_hi, la_lo, tri, ones, state, scores, go_ref):
    chunk = tri.shape[0]
    la_hi, la_lo = la_hi[rows], la_lo[rows]
    b_all = _dot(tri, la_hi) + _dot(tri, la_lo)
    gated = []
    for h in range(GLA_HEADS):
        ks = slice(h * GLA_DK, (h + 1) * GLA_DK)
        vs = slice(h * GLA_DV, (h + 1) * GLA_DV)
        b = b_all[:, ks]
        qh = q[rows, ks]
        kh = k[rows, ks]
        vb = v[rows, vs].astype(bf16)
        b_last = b[chunk - 1:chunk, :]
        q_dec = (qh * jnp.exp(b)).astype(bf16)
        k_end = (kh * jnp.exp(b_last - b)).astype(bf16)
        s_old = state[h]
        o_inter = _dot(q_dec, s_old.astype(bf16))

        for i in range(chunk // GLA_SUB):
            r0 = i * GLA_SUB
            nk = r0 + GLA_SUB
            npad = LANES * (-(-nk // LANES))
            bq = b[r0:nk]
            bk = b[0:nk]
            if i == 0:
                q_exp, k_exp = bq, -bk
            else:
                ref = b[r0 - 1:r0, :]
                q_exp, k_exp = bq - ref, ref - bk
            qi = (qh[r0:nk] * jnp.exp(q_exp)).astype(bf16)
            ki = (kh[0:nk] * jnp.exp(k_exp)).astype(bf16)
            if nk < npad:
                ki = jnp.concatenate([ki, jnp.zeros((npad - nk, GLA_DK), bf16)], axis=0)
            s = _dot_t_rhs(qi, ki)
            causal = (lax.broadcasted_iota(jnp.int32, (GLA_SUB, npad), 1)
                      <= lax.broadcasted_iota(jnp.int32, (GLA_SUB, npad), 0) + r0)
            scores[h, r0:nk, 0:npad] = jnp.where(causal, s, 0.0).astype(bf16)
            if npad < chunk:
                scores[h, r0:nk, npad:chunk] = jnp.zeros((GLA_SUB, chunk - npad), bf16)

        o = o_inter + _dot(scores[h], vb)
        gated.append(_gla_gate(o, r[rows, vs], go_ref))
        state[h] = s_old * _decay_columns(la_hi[:, ks], la_lo[:, ks], ones) + _dot_t_lhs(k_end, vb)
    return jnp.concatenate(gated, axis=1)


def _gla_prompt(x, states, w, layer, j, prev, batch, seq):
    chunk, seqs = GLA_CHUNK, GLA_SEQS
    n_layers = states.shape[0]
    first = prev is None
    extra_specs, extra_args, aliases = _stacked(prev, 12, 1)
    seq_tile = pl.BlockSpec((seqs, chunk, D_MODEL), lambda i, c: (i, c, 0))
    state_shape = (seqs, GLA_HEADS, GLA_DK, GLA_DV)
    out, new_states = pl.pallas_call(
        functools.partial(_gla_prompt_body, layer=j, first=first),
        grid=(batch // seqs, seq // chunk),
        in_specs=[seq_tile, _vec(layer),
                  _mat(j, D_MODEL, GLA_DKT, 0), _mat(j, D_MODEL, GLA_DKT, 1),
                  _mat(j, D_MODEL, GLA_DVT, 1), _mat(j, D_MODEL, GLA_DVT, 2),
                  _mat(j, D_MODEL, LANES), _mat(j, LANES, GLA_DKT), _vec(j, GLA_DKT),
                  pl.BlockSpec((None,) + state_shape, lambda i, c: (j, i, 0, 0, 0)),
                  _vec(j, GLA_DV), _mat(j, GLA_DVT, D_MODEL)] + extra_specs,
        out_specs=[seq_tile, _layer_block(first, n_layers, j, state_shape, lambda i, c: (i, 0, 0, 0))],
        out_shape=[jax.ShapeDtypeStruct((batch, seq, D_MODEL), f32),
                   jax.ShapeDtypeStruct((n_layers, batch, GLA_HEADS, GLA_DK, GLA_DV), f32)],
        scratch_shapes=[pltpu.VMEM(state_shape, f32), pltpu.VMEM((seqs, GLA_HEADS, chunk, chunk), bf16)],
        input_output_aliases=aliases,
        compiler_params=_params("arbitrary", "arbitrary"),
        name="gla_prompt",
    )(x.reshape(batch, seq, D_MODEL), w["norm_mix"], w["gla_w_in"], w["gla_w_in"], w["gla_w_in"], w["gla_w_in"],
      w["gla_w_z"], w["gla_w_gate2"], w["gla_b_gate"], states, w["gla_onorm_g"], w["gla_w_out"], *extra_args)
    return out.reshape(batch * seq, D_MODEL), new_states


def _gla_b_sample_body(q_ref, k_ref, v_ref, la_ref, s0_ref, *rest, layer, first):
    o_ref, so_ref = rest[-2:]
    so = _own_layer(so_ref, layer, first)
    steps, bb, _ = q_ref.shape
    causal = (lax.broadcasted_iota(jnp.int32, (steps, steps), 1)
              <= lax.broadcasted_iota(jnp.int32, (steps, steps), 0))
    ones = jnp.ones((steps, LANES), bf16)

    for i in range(bb):
        la = la_ref[:, i, :]
        q, k, v = q_ref[:, i, :], k_ref[:, i, :], v_ref[:, i, :]
        cum = [la[0:1]]
        for t in range(1, steps):
            cum.append(cum[-1] + la[t:t + 1])
        b_all = jnp.concatenate(cum, axis=0)
        la_hi, la_lo = _split_bf16(la)
        heads = []
        for h in range(GLA_HEADS):
            ks = slice(h * GLA_DK, (h + 1) * GLA_DK)
            vs = slice(h * GLA_DV, (h + 1) * GLA_DV)
            b = b_all[:, ks]
            qh = q[:, ks]
            kh = k[:, ks]
            vb = v[:, vs].astype(bf16)
            b_last = b[steps - 1:steps, :]
            q_dec = (qh * jnp.exp(b)).astype(bf16)
            k_inv = (kh * jnp.exp(-b)).astype(bf16)
            k_end = (kh * jnp.exp(b_last - b)).astype(bf16)
            s_old = s0_ref[i, h]
            sc = jnp.where(causal, _dot_t_rhs(q_dec, k_inv), 0.0).astype(bf16)
            heads.append(_dot(sc, vb) + _dot(q_dec, s_old.astype(bf16)))
            so[i, h] = s_old * _decay_columns(la_hi[:, ks], la_lo[:, ks], ones) + _dot_t_lhs(k_end, vb)
        o_ref[:, i, :] = jnp.concatenate(heads, axis=1)


def _gla_b_sample(q, k, v, la, states, j, prev):
    steps, batch, _ = q.shape
    bb = GLA_SAMPLE_BATCH
    n_layers = states.shape[0]
    first = prev is None
    extra_specs, extra_args, aliases = _stacked(prev, 5, 1)
    step_block = lambda n: pl.BlockSpec((steps, bb, n), lambda i: (0, i, 0))
    state_shape = (bb, GLA_HEADS, GLA_DK, GLA_DV)
    return pl.pallas_call(
        functools.partial(_gla_b_sample_body, layer=j, first=first),
        grid=(batch // bb,),
        in_specs=[step_block(GLA_DKT), step_block(GLA_DKT), step_block(GLA_DVT), step_block(GLA_DKT),
                  pl.BlockSpec((None,) + state_shape, lambda i: (j, i, 0, 0, 0))] + extra_specs,
        out_specs=[step_block(GLA_DVT),
                   _layer_block(first, n_layers, j, state_shape, lambda i: (i, 0, 0, 0))],
        out_shape=[jax.ShapeDtypeStruct((steps, batch, GLA_DVT), f32),
                   jax.ShapeDtypeStruct((n_layers, batch, GLA_HEADS, GLA_DK, GLA_DV), f32)],
        input_output_aliases=aliases,
        compiler_params=_params("arbitrary"),
        name="gla_core_sample",
    )(q, k, v, la, states, *extra_args)


def _gla_gate(o, r, go_ref):
    return (_rms(o, go_ref[...]) * _silu(r)).astype(bf16)


def _gla_c_body(o_ref, r_ref, x_ref, go_ref, wo_ref, out_ref):
    parts = []
    for h in range(GLA_HEADS):
        vs = slice(h * GLA_DV, (h + 1) * GLA_DV)
        parts.append(_gla_gate(o_ref[:, vs], r_ref[:, vs], go_ref))
    out_ref[...] = x_ref[...] + _dot(jnp.concatenate(parts, axis=1), wo_ref[...])


def _gla_c(o, r, x, w, j):
    rows = x.shape[0]
    tile = min(ROW_TILE, rows)
    return pl.pallas_call(
        _gla_c_body,
        grid=(rows // tile,),
        in_specs=[_rows(tile, GLA_DVT), _rows(tile, GLA_DVT), _rows(tile, D_MODEL), _vec(j, GLA_DV),
                  _mat(j, GLA_DVT, D_MODEL)],
        out_specs=_rows(tile, D_MODEL),
        out_shape=jax.ShapeDtypeStruct((rows, D_MODEL), f32),
        compiler_params=_params("arbitrary"),
        name="gla_out",
    )(o, r, x, w["gla_onorm_g"], w["gla_w_out"])


def _mix_sample(x, conv_state, gla_state, w, i, new_conv, new_gla, batch, steps):
    j = i // 2
    if i % 2 == 0:
        u = _conv_a(x, w, i, j)
        c, new_conv = _conv_b_sample(u.reshape(steps, batch, D_MODEL), conv_state, w, j, new_conv)
        x = _conv_c(c.reshape(steps * batch, D_MODEL), x, w, j)
    else:
        q, k, v, r, la = _gla_a(x, w, i, j)
        by_step = lambda a: a.reshape(steps, batch, a.shape[-1])
        o, new_gla = _gla_b_sample(by_step(q), by_step(k), by_step(v), by_step(la), gla_state, j, new_gla)
        x = _gla_c(o.reshape(steps * batch, GLA_DVT), r, x, w, j)
    return x, new_conv, new_gla


def _mix_prompt(x, conv_state, gla_state, w, i, new_conv, new_gla, batch, seq):
    j = i // 2
    if i % 2 == 0:
        x, new_conv = _conv_prompt(x, conv_state, w, i, j, new_conv, batch, seq)
    else:
        x, new_gla = _gla_prompt(x, gla_state, w, i, j, new_gla, batch, seq)
    return x, new_conv, new_gla


def kernel(x_prompt, x_sample, state_conv, state_gla, norm_ffn_pre, norm_mix, norm_ffn_post, norm_final,
           ffn_w_gate, ffn_w_up, ffn_w_down, conv_w_pw1, conv_b_pw1, conv_w_dw, conv_b_dw, conv_ln_g, conv_ln_b,
           conv_w_pw2, conv_b_pw2, gla_w_in, gla_w_gate2, gla_b_gate, gla_onorm_g, gla_w_out):
    row = lambda a: a[..., None, :]
    z_start = 2 * GLA_DKT + 2 * GLA_DVT
    w_z = jnp.pad(gla_w_in[:, :, z_start:], ((0, 0), (0, 0), (0, LANES - GLA_GATE_RANK)))
    w_gate2 = jnp.pad(gla_w_gate2, ((0, 0), (0, LANES - GLA_GATE_RANK), (0, 0)))
    w = dict(
        norm_ffn_pre=row(norm_ffn_pre), norm_mix=row(norm_mix), norm_ffn_post=row(norm_ffn_post),
        norm_final=norm_final[None, :],
        ffn_w_gate=ffn_w_gate, ffn_w_up=ffn_w_up, ffn_w_down=ffn_w_down,
        conv_w_pw1=conv_w_pw1.astype(bf16), conv_b_pw1=row(conv_b_pw1),
        conv_w_dw=conv_w_dw, conv_b_dw=row(conv_b_dw), conv_ln_g=row(conv_ln_g), conv_ln_b=row(conv_ln_b),
        conv_w_pw2=conv_w_pw2.astype(bf16), conv_b_pw2=row(conv_b_pw2),
        gla_w_in=gla_w_in.astype(bf16), gla_w_z=w_z.astype(bf16), gla_w_gate2=w_gate2.astype(bf16),
        gla_b_gate=row(gla_b_gate), gla_onorm_g=row(gla_onorm_g), gla_w_out=gla_w_out.astype(bf16),
    )
    batch, seq, _ = x_prompt.shape
    dec_batch, steps, _ = x_sample.shape
    conv0 = jnp.zeros((state_conv.shape[0], batch) + state_conv.shape[2:], x_prompt.dtype)
    gla0 = jnp.zeros((state_gla.shape[0], batch) + state_gla.shape[2:], x_prompt.dtype)
    conv_in_s = state_conv.transpose(0, 2, 1, 3)
    xp = x_prompt.reshape(batch * seq, D_MODEL)
    xs = x_sample.transpose(1, 0, 2).reshape(steps * dec_batch, D_MODEL)

    wb = tuple(a[0, 0].astype(bf16) for a in (ffn_w_gate, ffn_w_up, ffn_w_down))
    conv_p = gla_p = conv_s = gla_s = None
    for i in range(DEPTH):
        xp, xs, wb = _ffn(xp, xs, wb, w, "norm_ffn_pre", i, (i, 1))
        xp, conv_p, gla_p = _mix_prompt(xp, conv0, gla0, w, i, conv_p, gla_p, batch, seq)
        xs, conv_s, gla_s = _mix_sample(xs, conv_in_s, state_gla, w, i, conv_s, gla_s, dec_batch, steps)
        last = i == DEPTH - 1
        xp, xs, wb = _ffn(xp, xs, wb, w, "norm_ffn_post", i, None if last else (i + 1, 0), final_norm=last)

    y_p = xp.reshape(batch, seq, D_MODEL)
    y_s = xs.reshape(steps, dec_batch, D_MODEL).transpose(1, 0, 2)
    return (y_p, y_s, conv_p, gla_p, conv_s.transpose(0, 2, 1, 3), gla_s)
```

```python
import functools

import jax
import jax.numpy as jnp
from jax import lax
from jax.experimental import pallas as pl
from jax.experimental.pallas import tpu as pltpu

f32 = jnp.float32
bf16 = jnp.bfloat16

D_MODEL = 1024
D_FF = 2816
DEPTH = 4
CONV_WIDTH = 31
CONV_STATE = CONV_WIDTH - 1
GLA_HEADS = 4
GLA_DK = 128
GLA_DV = 256
GLA_DKT = GLA_HEADS * GLA_DK
GLA_DVT = GLA_HEADS * GLA_DV
GLA_GATE_RANK = 16
GLA_TAU = 16.0
EPS = 1e-6

LANES = 128
SUBLANES = 8
N_SLABS = D_MODEL // LANES
VMEM_LIMIT = 56 * 1024 * 1024

ROW_TILE = 512
FFN_TILE = 1024
FFN_CHUNK = 256
CONV_TILE = 512
CONV_HALO = 32
CONV_STRIDE = 4
CONV_BLOCK = SUBLANES * CONV_STRIDE
CONV_SAMPLE_BATCH = 8
GLA_CHUNK = 256
GLA_SEQS = 2
GLA_SUB = 32
GLA_SAMPLE_BATCH = 8


def _params(*semantics):
    return pltpu.CompilerParams(dimension_semantics=semantics, vmem_limit_bytes=VMEM_LIMIT)


def _resident(block, index=None):
    index = (0,) * len(block) if index is None else index
    return pl.BlockSpec(block, lambda *_: index, pipeline_mode=pl.Buffered(1))


def _vec(layer, width=D_MODEL, part=0):
    return _resident((None, 1, width), (layer, 0, part))


def _mat(layer, rows, cols, part=0):
    return _resident((None, rows, cols), (layer, 0, part))


def _rows(tile, width):
    return pl.BlockSpec((tile, width), lambda i: (i, 0))


def _stacked(prev, in_count, out_index):
    if prev is None:
        return [], [], {}
    return [pl.BlockSpec(memory_space=pl.ANY)], [prev], {in_count: out_index}


def _layer_block(first, n_layers, layer, block, index):
    lead, at = (n_layers, 0) if first else (None, layer)
    return pl.BlockSpec((lead,) + block, lambda *grid: (at,) + index(*grid))


def _own_layer(ref, layer, first):
    if not first:
        return ref
    for other in range(ref.shape[0]):
        if other != layer:
            ref[other] = jnp.zeros(ref.shape[1:], ref.dtype)
    return ref.at[layer]


def _rms(x, g):
    return x * lax.rsqrt(jnp.mean(x * x, axis=-1, keepdims=True) + EPS) * g


def _silu(x):
    return x * jax.nn.sigmoid(x)


def _dot(a, b):
    return jnp.dot(a, b, preferred_element_type=f32)


def _dot_t_rhs(a, b):
    return lax.dot_general(a, b, (((1,), (1,)), ((), ())), preferred_element_type=f32)


def _dot_t_lhs(a, b):
    return lax.dot_general(a, b, (((0,), (0,)), ((), ())), preferred_element_type=f32)


def _ffn_rows(x, g_ref, wg_ref, wu_ref, wd_ref, gf_ref, a_scr, final_norm):
    rows = x.shape[0]
    h = _rms(x, g_ref[...]).astype(bf16)
    for j in range(D_FF // FFN_CHUNK):
        cols = pl.ds(j * FFN_CHUNK, FFN_CHUNK)
        gate = _dot(h, wg_ref[:, cols])
        up = _dot(h, wu_ref[:, cols])
        a_scr[0:rows, cols] = (_silu(gate) * up).astype(bf16)
    y = x + 0.5 * _dot(a_scr[0:rows, :], wd_ref[...])
    return _rms(y, gf_ref[...]) if final_norm else y


def _ffn_body(xp_ref, xs_ref, g_ref, wg_ref, wu_ref, wd_ref, gf_ref, *rest, final_norm, cast_next):
    op_ref, os_ref = rest[3:5] if cast_next else rest[0:2]
    a_scr = rest[-1]
    weights = (g_ref, wg_ref, wu_ref, wd_ref, gf_ref, a_scr, final_norm)
    op_ref[...] = _ffn_rows(xp_ref[...], *weights)

    @pl.when(pl.program_id(0) == pl.num_programs(0) - 1)
    def _():
        os_ref[...] = _ffn_rows(xs_ref[...], *weights)

    if cast_next:
        for src, dst in zip(rest[0:3], rest[5:8]):
            dst[...] = src[...].astype(bf16)


def _ffn(xp, xs, wb, w, gain, layer, nxt, *, final_norm=False):
    rows = xp.shape[0]
    tile = FFN_TILE
    steps = rows // tile
    cast_next = nxt is not None
    in_specs = [_rows(tile, D_MODEL), _resident(xs.shape), _vec(layer), _resident((D_MODEL, D_FF)),
                _resident((D_MODEL, D_FF)), _resident((D_FF, D_MODEL)), _resident((1, D_MODEL))]
    out_specs = [_rows(tile, D_MODEL), pl.BlockSpec(xs.shape, lambda i: (0, 0))]
    out_shape = [jax.ShapeDtypeStruct(xp.shape, f32), jax.ShapeDtypeStruct(xs.shape, f32)]
    args = [xp, xs, w[gain], *wb, w["norm_final"]]
    if cast_next:
        for name, (r, c) in (("ffn_w_gate", (D_MODEL, D_FF)), ("ffn_w_up", (D_MODEL, D_FF)),
                             ("ffn_w_down", (D_FF, D_MODEL))):
            in_specs.append(pl.BlockSpec((None, None, r // steps, c), lambda i: nxt + (i, 0)))
            out_specs.append(pl.BlockSpec((r // steps, c), lambda i: (i, 0)))
            out_shape.append(jax.ShapeDtypeStruct((r, c), bf16))
            args.append(w[name])
    outs = pl.pallas_call(
        functools.partial(_ffn_body, final_norm=final_norm, cast_next=cast_next),
        grid=(steps,),
        in_specs=in_specs,
        out_specs=out_specs,
        out_shape=out_shape,
        scratch_shapes=[pltpu.VMEM((tile, D_FF), bf16)],
        compiler_params=_params("arbitrary"),
        name="ffn",
    )(*args)
    return outs[0], outs[1], (tuple(outs[2:]) if cast_next else None)


def _conv_glu(x, g_ref, wa_ref, wb_ref, ba_ref, bb_ref):
    h = _rms(x, g_ref[...]).astype(bf16)
    a = _dot(h, wa_ref[...]) + ba_ref[...]
    gate = _dot(h, wb_ref[...]) + bb_ref[...]
    return a * jax.nn.sigmoid(gate)


def _conv_a_body(x_ref, g_ref, wa_ref, wb_ref, ba_ref, bb_ref, u_ref):
    u_ref[...] = _conv_glu(x_ref[...], g_ref, wa_ref, wb_ref, ba_ref, bb_ref)


def _conv_a(x, w, layer, j):
    rows = x.shape[0]
    tile = min(ROW_TILE, rows)
    return pl.pallas_call(
        _conv_a_body,
        grid=(rows // tile,),
        in_specs=[_rows(tile, D_MODEL), _vec(layer), _mat(j, D_MODEL, D_MODEL, 0), _mat(j, D_MODEL, D_MODEL, 1),
                  _vec(j, part=0), _vec(j, part=1)],
        out_specs=_rows(tile, D_MODEL),
        out_shape=jax.ShapeDtypeStruct((rows, D_MODEL), f32),
        compiler_params=_params("arbitrary"),
        name="conv_glu",
    )(x, w["norm_mix"], w["conv_w_pw1"], w["conv_w_pw1"], w["conv_b_pw1"], w["conv_b_pw1"])


def _conv_prompt_body(x_ref, g_ref, wa_ref, wb_ref, ba_ref, bb_ref, st_ref, w_ref, b_ref,
                      lg_ref, lb_ref, w2_ref, b2_ref, *rest, layer, first):
    o_ref, ns_ref, hist, cout, taps = rest[-5:]
    t = pl.program_id(1)
    tile = x_ref.shape[0]

    @pl.when(t == 0)
    def _():
        for l in range(N_SLABS):
            lanes = slice(l * LANES, (l + 1) * LANES)
            hist[l, CONV_HALO - CONV_STATE:CONV_HALO, :] = st_ref[0, :, lanes]
            for j in range(CONV_WIDTH):
                taps[l, j] = jnp.broadcast_to(w_ref[j:j + 1, lanes], (2 * SUBLANES, LANES)).astype(bf16)

    @pl.when(t > 0)
    def _():
        for l in range(N_SLABS):
            hist[l, 0:CONV_HALO, :] = hist[l, tile:tile + CONV_HALO, :]

    x = x_ref[...]
    u = _conv_glu(x, g_ref, wa_ref, wb_ref, ba_ref, bb_ref)
    first_tap = CONV_HALO - CONV_STATE
    for l in range(N_SLABS):
        lanes = slice(l * LANES, (l + 1) * LANES)
        hist[l, CONV_HALO:CONV_HALO + tile, :] = u[:, lanes]

        def block_pair(pair, carry, l=l):
            bases = [(pair * 2 + half) * CONV_BLOCK for half in range(2)]
            acc = [None] * CONV_STRIDE
            for shift in range(first_tap, first_tap + CONV_WIDTH + CONV_STRIDE - 1):
                rows = jnp.concatenate(
                    [hist[l, pl.ds(base + shift, SUBLANES, stride=CONV_STRIDE), :] for base in bases],
                    axis=0).astype(bf16)
                for t0 in range(CONV_STRIDE):
                    j = shift - t0 - first_tap
                    if 0 <= j < CONV_WIDTH:
                        term = rows.astype(f32) * taps[l, j].astype(f32)
                        acc[t0] = term if acc[t0] is None else acc[t0] + term
            for t0 in range(CONV_STRIDE):
                for half, base in enumerate(bases):
                    cout[l, pl.ds(base + t0, SUBLANES, stride=CONV_STRIDE), :] = (
                        acc[t0][half * SUBLANES:(half + 1) * SUBLANES])
            return carry

        lax.fori_loop(0, tile // (2 * CONV_BLOCK), block_pair, 0)

    c = jnp.concatenate([cout[l] for l in range(N_SLABS)], axis=1) + b_ref[...]
    o_ref[...] = _conv_tail(c, x, lg_ref, lb_ref, w2_ref, b2_ref)
    _own_layer(ns_ref, layer, first)[0] = jnp.concatenate(
        [hist[l, tile + CONV_HALO - CONV_STATE:tile + CONV_HALO, :] for l in range(N_SLABS)], axis=1)


def _conv_prompt(x, states, w, layer, j, prev, batch, seq):
    tile = CONV_TILE
    nt = seq // tile
    n_layers = states.shape[0]
    extra_specs, extra_args, aliases = _stacked(prev, 13, 1)
    seq_tile = pl.BlockSpec((tile, D_MODEL), lambda i, t: (i * nt + t, 0))
    state_block = pl.BlockSpec((None, 1, CONV_STATE, D_MODEL), lambda i, t: (j, i, 0, 0))
    return pl.pallas_call(
        functools.partial(_conv_prompt_body, layer=j, first=prev is None),
        grid=(batch, nt),
        in_specs=[seq_tile, _vec(layer), _mat(j, D_MODEL, D_MODEL, 0), _mat(j, D_MODEL, D_MODEL, 1),
                  _vec(j, part=0), _vec(j, part=1), state_block, _mat(j, CONV_WIDTH, D_MODEL), _vec(j),
                  _vec(j), _vec(j), _mat(j, D_MODEL, D_MODEL), _vec(j)] + extra_specs,
        out_specs=[seq_tile, _layer_block(prev is None, n_layers, j, (1, CONV_STATE, D_MODEL),
                                          lambda i, t: (i, 0, 0))],
        out_shape=[jax.ShapeDtypeStruct((batch * seq, D_MODEL), f32),
                   jax.ShapeDtypeStruct((n_layers, batch, CONV_STATE, D_MODEL), f32)],
        scratch_shapes=[pltpu.VMEM((N_SLABS, CONV_HALO + tile, LANES), f32),
                        pltpu.VMEM((N_SLABS, tile, LANES), f32),
                        pltpu.VMEM((N_SLABS, CONV_WIDTH, 2 * SUBLANES, LANES), bf16)],
        input_output_aliases=aliases,
        compiler_params=_params("arbitrary", "arbitrary"),
        name="conv_prompt",
    )(x, w["norm_mix"], w["conv_w_pw1"], w["conv_w_pw1"], w["conv_b_pw1"], w["conv_b_pw1"], states,
      w["conv_w_dw"], w["conv_b_dw"], w["conv_ln_g"], w["conv_ln_b"], w["conv_w_pw2"], w["conv_b_pw2"],
      *extra_args)


def _conv_b_sample_body(u_ref, st_ref, w_ref, b_ref, *rest, layer, first):
    c_ref, ns_ref = rest[-2:]
    ns = _own_layer(ns_ref, layer, first)
    steps = u_ref.shape[0]
    acc = [None] * steps
    for i in range(CONV_STATE + steps):
        row = st_ref[i] if i < CONV_STATE else u_ref[i - CONV_STATE]
        for t in range(steps):
            j = i - t
            if 0 <= j < CONV_WIDTH and i <= CONV_STATE + t:
                term = row * w_ref[j:j + 1, :]
                acc[t] = term if acc[t] is None else acc[t] + term
        if i >= steps:
            ns[i - steps] = row
    for t in range(steps):
        c_ref[t] = acc[t] + b_ref[...]


def _conv_b_sample(u3, states, w, j, prev):
    steps, batch, _ = u3.shape
    bb = CONV_SAMPLE_BATCH
    n_layers = states.shape[0]
    first = prev is None
    extra_specs, extra_args, aliases = _stacked(prev, 4, 1)
    step_block = pl.BlockSpec((steps, bb, D_MODEL), lambda i: (0, i, 0))
    return pl.pallas_call(
        functools.partial(_conv_b_sample_body, layer=j, first=first),
        grid=(batch // bb,),
        in_specs=[step_block, pl.BlockSpec((None, CONV_STATE, bb, D_MODEL), lambda i: (j, 0, i, 0)),
                  _mat(j, CONV_WIDTH, D_MODEL), _vec(j)] + extra_specs,
        out_specs=[step_block,
                   _layer_block(first, n_layers, j, (CONV_STATE, bb, D_MODEL), lambda i: (0, i, 0))],
        out_shape=[jax.ShapeDtypeStruct((steps, batch, D_MODEL), f32),
                   jax.ShapeDtypeStruct((n_layers, CONV_STATE, batch, D_MODEL), f32)],
        input_output_aliases=aliases,
        compiler_params=_params("arbitrary"),
        name="conv_dw_sample",
    )(u3, states, w["conv_w_dw"], w["conv_b_dw"], *extra_args)


def _conv_tail(c, x, lg_ref, lb_ref, w2_ref, b2_ref):
    d = c - jnp.mean(c, axis=-1, keepdims=True)
    var = jnp.mean(d * d, axis=-1, keepdims=True)
    y = _silu(d * lax.rsqrt(var + EPS) * lg_ref[...] + lb_ref[...])
    return x + _dot(y.astype(bf16), w2_ref[...]) + b2_ref[...]


def _conv_c_body(c_ref, x_ref, lg_ref, lb_ref, w2_ref, b2_ref, o_ref):
    o_ref[...] = _conv_tail(c_ref[...], x_ref[...], lg_ref, lb_ref, w2_ref, b2_ref)


def _conv_c(c, x, w, j):
    rows = x.shape[0]
    tile = min(ROW_TILE, rows)
    return pl.pallas_call(
        _conv_c_body,
        grid=(rows // tile,),
        in_specs=[_rows(tile, D_MODEL), _rows(tile, D_MODEL), _vec(j), _vec(j), _mat(j, D_MODEL, D_MODEL), _vec(j)],
        out_specs=_rows(tile, D_MODEL),
        out_shape=jax.ShapeDtypeStruct((rows, D_MODEL), f32),
        compiler_params=_params("arbitrary"),
        name="conv_out",
    )(c, x, w["conv_ln_g"], w["conv_ln_b"], w["conv_w_pw2"], w["conv_b_pw2"])


def _gla_a_body(x_ref, g_ref, wq_ref, wk_ref, wv_ref, wr_ref, wz_ref, wg2_ref, bg_ref,
                q_ref, k_ref, v_ref, r_ref, la_ref):
    q, k, v, r, la = _gla_project(x_ref[...], g_ref, wq_ref, wk_ref, wv_ref, wr_ref, wz_ref, wg2_ref, bg_ref)
    q_ref[...] = q
    k_ref[...] = k
    v_ref[...] = v
    r_ref[...] = r
    la_ref[...] = la


def _gla_project(x, g_ref, wq_ref, wk_ref, wv_ref, wr_ref, wz_ref, wg2_ref, bg_ref):
    h = _rms(x, g_ref[...]).astype(bf16)
    q = _dot(h, wq_ref[...]) * (GLA_DK ** -0.5)
    k = _dot(h, wk_ref[...])
    v = _dot(h, wv_ref[...])
    r = _dot(h, wr_ref[...])
    z = _dot(h, wz_ref[...])
    pre = _dot(z.astype(bf16), wg2_ref[...]) + bg_ref[...]
    la = (jnp.minimum(pre, 0.0) - jnp.log1p(jnp.exp(-jnp.abs(pre)))) * (1.0 / GLA_TAU)
    return q, k, v, r, la


def _gla_a(x, w, layer, j):
    rows = x.shape[0]
    tile = min(ROW_TILE, rows)
    widths = (GLA_DKT, GLA_DKT, GLA_DVT, GLA_DVT, GLA_DKT)
    return pl.pallas_call(
        _gla_a_body,
        grid=(rows // tile,),
        in_specs=[_rows(tile, D_MODEL), _vec(layer),
                  _mat(j, D_MODEL, GLA_DKT, 0), _mat(j, D_MODEL, GLA_DKT, 1),
                  _mat(j, D_MODEL, GLA_DVT, 1), _mat(j, D_MODEL, GLA_DVT, 2),
                  _mat(j, D_MODEL, LANES), _mat(j, LANES, GLA_DKT), _vec(j, GLA_DKT)],
        out_specs=[_rows(tile, n) for n in widths],
        out_shape=[jax.ShapeDtypeStruct((rows, n), f32) for n in widths],
        compiler_params=_params("arbitrary"),
        name="gla_proj",
    )(x, w["norm_mix"], w["gla_w_in"], w["gla_w_in"], w["gla_w_in"], w["gla_w_in"], w["gla_w_z"],
      w["gla_w_gate2"], w["gla_b_gate"])


def _split_bf16(x):
    hi = x.astype(bf16)
    return hi, (x - hi.astype(f32)).astype(bf16)


def _decay_columns(la_hi, la_lo, ones):
    total = _dot_t_lhs(la_hi, ones) + _dot_t_lhs(la_lo, ones)
    return jnp.concatenate([jnp.exp(total)] * (GLA_DV // LANES), axis=1)


def _gla_prompt_body(x_ref, g_ref, wq_ref, wk_ref, wv_ref, wr_ref, wz_ref, wg2_ref, bg_ref, s0_ref,
                     go_ref, wo_ref, *rest, layer, first):
    out_ref, so_ref, state, scores = rest[-4:]
    c = pl.program_id(1)
    seqs, chunk, _ = x_ref.shape

    @pl.when(c == 0)
    def _():
        state[...] = s0_ref[...]

    x = x_ref[...].reshape(seqs * chunk, D_MODEL)
    q, k, v, r, la = _gla_project(x, g_ref, wq_ref, wk_ref, wv_ref, wr_ref, wz_ref, wg2_ref, bg_ref)
    la_hi, la_lo = _split_bf16(la)
    tri = (lax.broadcasted_iota(jnp.int32, (chunk, chunk), 1)
           <= lax.broadcasted_iota(jnp.int32, (chunk, chunk), 0)).astype(bf16)
    ones = jnp.ones((chunk, LANES), bf16)
    gated = [_gla_chunk(slice(s * chunk, (s + 1) * chunk), q, k, v, r, la_hi, la_lo, tri, ones,
                        state.at[s], scores.at[s], go_ref) for s in range(seqs)]
    y = x + _dot(jnp.concatenate(gated, axis=0), wo_ref[...])
    out_ref[...] = y.reshape(seqs, chunk, D_MODEL)

    @pl.when(c == pl.num_programs(1) - 1)
    def _():
        _own_layer(so_ref, layer, first)[...] = state[...]


def _gla_chunk(rows, q, k, v, r, la_hi, la_lo, tri, ones, state, scores, go_ref):
    chunk = tri.shape[0]
    la_hi, la_lo = la_hi[rows], la_lo[rows]
    b_all = _dot(tri, la_hi) + _dot(tri, la_lo)
    gated = []
    for h in range(GLA_HEADS):
        ks = slice(h * GLA_DK, (h + 1) * GLA_DK)
        vs = slice(h * GLA_DV, (h + 1) * GLA_DV)
        b = b_all[:, ks]
        qh = q[rows, ks]
        kh = k[rows, ks]
        vb = v[rows, vs].astype(bf16)
        b_last = b[chunk - 1:chunk, :]
        q_dec = (qh * jnp.exp(b)).astype(bf16)
        k_end = (kh * jnp.exp(b_last - b)).astype(bf16)
        s_old = state[h]
        o_inter = _dot(q_dec, s_old.astype(bf16))

        for i in range(chunk // GLA_SUB):
            r0 = i * GLA_SUB
            nk = r0 + GLA_SUB
            npad = LANES * (-(-nk // LANES))
            bq = b[r0:nk]
            bk = b[0:nk]
            if i == 0:
                q_exp, k_exp = bq, -bk
            else:
                ref = b[r0 - 1:r0, :]
                q_exp, k_exp = bq - ref, ref - bk
            qi = (qh[r0:nk] * jnp.exp(q_exp)).astype(bf16)
            ki = (kh[0:nk] * jnp.exp(k_exp)).astype(bf16)
            if nk < npad:
                ki = jnp.concatenate([ki, jnp.zeros((npad - nk, GLA_DK), bf16)], axis=0)
            s = _dot_t_rhs(qi, ki)
            causal = (lax.broadcasted_iota(jnp.int32, (GLA_SUB, npad), 1)
                      <= lax.broadcasted_iota(jnp.int32, (GLA_SUB, npad), 0) + r0)
            scores[h, r0:nk, 0:npad] = jnp.where(causal, s, 0.0).astype(bf16)
            if npad < chunk:
                scores[h, r0:nk, npad:chunk] = jnp.zeros((GLA_SUB, chunk - npad), bf16)

        o = o_inter + _dot(scores[h], vb)
        gated.append(_gla_gate(o, r[rows, vs], go_ref))
        state[h] = s_old * _decay_columns(la_hi[:, ks], la_lo[:, ks], ones) + _dot_t_lhs(k_end, vb)
    return jnp.concatenate(gated, axis=1)


def _gla_prompt(x, states, w, layer, j, prev, batch, seq):
    chunk, seqs = GLA_CHUNK, GLA_SEQS
    n_layers = states.shape[0]
    first = prev is None
    extra_specs, extra_args, aliases = _stacked(prev, 12, 1)
    seq_tile = pl.BlockSpec((seqs, chunk, D_MODEL), lambda i, c: (i, c, 0))
    state_shape = (seqs, GLA_HEADS, GLA_DK, GLA_DV)
    out, new_states = pl.pallas_call(
        functools.partial(_gla_prompt_body, layer=j, first=first),
        grid=(batch // seqs, seq // chunk),
        in_specs=[seq_tile, _vec(layer),
                  _mat(j, D_MODEL, GLA_DKT, 0), _mat(j, D_MODEL, GLA_DKT, 1),
                  _mat(j, D_MODEL, GLA_DVT, 1), _mat(j, D_MODEL, GLA_DVT, 2),
                  _mat(j, D_MODEL, LANES), _mat(j, LANES, GLA_DKT), _vec(j, GLA_DKT),
                  pl.BlockSpec((None,) + state_shape, lambda i, c: (j, i, 0, 0, 0)),
                  _vec(j, GLA_DV), _mat(j, GLA_DVT, D_MODEL)] + extra_specs,
        out_specs=[seq_tile, _layer_block(first, n_layers, j, state_shape, lambda i, c: (i, 0, 0, 0))],
        out_shape=[jax.ShapeDtypeStruct((batch, seq, D_MODEL), f32),
                   jax.ShapeDtypeStruct((n_layers, batch, GLA_HEADS, GLA_DK, GLA_DV), f32)],
        scratch_shapes=[pltpu.VMEM(state_shape, f32), pltpu.VMEM((seqs, GLA_HEADS, chunk, chunk), bf16)],
        input_output_aliases=aliases,
        compiler_params=_params("arbitrary", "arbitrary"),
        name="gla_prompt",
    )(x.reshape(batch, seq, D_MODEL), w["norm_mix"], w["gla_w_in"], w["gla_w_in"], w["gla_w_in"], w["gla_w_in"],
      w["gla_w_z"], w["gla_w_gate2"], w["gla_b_gate"], states, w["gla_onorm_g"], w["gla_w_out"], *extra_args)
    return out.reshape(batch * seq, D_MODEL), new_states


def _gla_b_sample_body(q_ref, k_ref, v_ref, la_ref, s0_ref, *rest, layer, first):
    o_ref, so_ref = rest[-2:]
    so = _own_layer(so_ref, layer, first)
    steps, bb, _ = q_ref.shape
    causal = (lax.broadcasted_iota(jnp.int32, (steps, steps), 1)
              <= lax.broadcasted_iota(jnp.int32, (steps, steps), 0))
    ones = jnp.ones((steps, LANES), bf16)

    for i in range(bb):
        la = la_ref[:, i, :]
        q, k, v = q_ref[:, i, :], k_ref[:, i, :], v_ref[:, i, :]
        cum = [la[0:1]]
        for t in range(1, steps):
            cum.append(cum[-1] + la[t:t + 1])
        b_all = jnp.concatenate(cum, axis=0)
        la_hi, la_lo = _split_bf16(la)
        heads = []
        for h in range(GLA_HEADS):
            ks = slice(h * GLA_DK, (h + 1) * GLA_DK)
            vs = slice(h * GLA_DV, (h + 1) * GLA_DV)
            b = b_all[:, ks]
            qh = q[:, ks]
            kh = k[:, ks]
            vb = v[:, vs].astype(bf16)
            b_last = b[steps - 1:steps, :]
            q_dec = (qh * jnp.exp(b)).astype(bf16)
            k_inv = (kh * jnp.exp(-b)).astype(bf16)
            k_end = (kh * jnp.exp(b_last - b)).astype(bf16)
            s_old = s0_ref[i, h]
            sc = jnp.where(causal, _dot_t_rhs(q_dec, k_inv), 0.0).astype(bf16)
            heads.append(_dot(sc, vb) + _dot(q_dec, s_old.astype(bf16)))
            so[i, h] = s_old * _decay_columns(la_hi[:, ks], la_lo[:, ks], ones) + _dot_t_lhs(k_end, vb)
        o_ref[:, i, :] = jnp.concatenate(heads, axis=1)


def _gla_b_sample(q, k, v, la, states, j, prev):
    steps, batch, _ = q.shape
    bb = GLA_SAMPLE_BATCH
    n_layers = states.shape[0]
    first = prev is None
    extra_specs, extra_args, aliases = _stacked(prev, 5, 1)
    step_block = lambda n: pl.BlockSpec((steps, bb, n), lambda i: (0, i, 0))
    state_shape = (bb, GLA_HEADS, GLA_DK, GLA_DV)
    return pl.pallas_call(
        functools.partial(_gla_b_sample_body, layer=j, first=first),
        grid=(batch // bb,),
        in_specs=[step_block(GLA_DKT), step_block(GLA_DKT), step_block(GLA_DVT), step_block(GLA_DKT),
                  pl.BlockSpec((None,) + state_shape, lambda i: (j, i, 0, 0, 0))] + extra_specs,
        out_specs=[step_block(GLA_DVT),
                   _layer_block(first, n_layers, j, state_shape, lambda i: (i, 0, 0, 0))],
        out_shape=[jax.ShapeDtypeStruct((steps, batch, GLA_DVT), f32),
                   jax.ShapeDtypeStruct((n_layers, batch, GLA_HEADS, GLA_DK, GLA_DV), f32)],
        input_output_aliases=aliases,
        compiler_params=_params("arbitrary"),
        name="gla_core_sample",
    )(q, k, v, la, states, *extra_args)


def _gla_gate(o, r, go_ref):
    return (_rms(o, go_ref[...]) * _silu(r)).astype(bf16)


def _gla_c_body(o_ref, r_ref, x_ref, go_ref, wo_ref, out_ref):
    parts = []
    for h in range(GLA_HEADS):
        vs = slice(h * GLA_DV, (h + 1) * GLA_DV)
        parts.append(_gla_gate(o_ref[:, vs], r_ref[:, vs], go_ref))
    out_ref[...] = x_ref[...] + _dot(jnp.concatenate(parts, axis=1), wo_ref[...])


def _gla_c(o, r, x, w, j):
    rows = x.shape[0]
    tile = min(ROW_TILE, rows)
    return pl.pallas_call(
        _gla_c_body,
        grid=(rows // tile,),
        in_specs=[_rows(tile, GLA_DVT), _rows(tile, GLA_DVT), _rows(tile, D_MODEL), _vec(j, GLA_DV),
                  _mat(j, GLA_DVT, D_MODEL)],
        out_specs=_rows(tile, D_MODEL),
        out_shape=jax.ShapeDtypeStruct((rows, D_MODEL), f32),
        compiler_params=_params("arbitrary"),
        name="gla_out",
    )(o, r, x, w["gla_onorm_g"], w["gla_w_out"])


def _mix_sample(x, conv_state, gla_state, w, i, new_conv, new_gla, batch, steps):
    j = i // 2
    if i % 2 == 0:
        u = _conv_a(x, w, i, j)
        c, new_conv = _conv_b_sample(u.reshape(steps, batch, D_MODEL), conv_state, w, j, new_conv)
        x = _conv_c(c.reshape(steps * batch, D_MODEL), x, w, j)
    else:
        q, k, v, r, la = _gla_a(x, w, i, j)
        by_step = lambda a: a.reshape(steps, batch, a.shape[-1])
        o, new_gla = _gla_b_sample(by_step(q), by_step(k), by_step(v), by_step(la), gla_state, j, new_gla)
        x = _gla_c(o.reshape(steps * batch, GLA_DVT), r, x, w, j)
    return x, new_conv, new_gla


def _mix_prompt(x, conv_state, gla_state, w, i, new_conv, new_gla, batch, seq):
    j = i // 2
    if i % 2 == 0:
        x, new_conv = _conv_prompt(x, conv_state, w, i, j, new_conv, batch, seq)
    else:
        x, new_gla = _gla_prompt(x, gla_state, w, i, j, new_gla, batch, seq)
    return x, new_conv, new_gla


def kernel(x_prompt, x_sample, state_conv, state_gla, norm_ffn_pre, norm_mix, norm_ffn_post, norm_final,
           ffn_w_gate, ffn_w_up, ffn_w_down, conv_w_pw1, conv_b_pw1, conv_w_dw, conv_b_dw, conv_ln_g, conv_ln_b,
           conv_w_pw2, conv_b_pw2, gla_w_in, gla_w_gate2, gla_b_gate, gla_onorm_g, gla_w_out):
    row = lambda a: a[..., None, :]
    z_start = 2 * GLA_DKT + 2 * GLA_DVT
    w_z = jnp.pad(gla_w_in[:, :, z_start:], ((0, 0), (0, 0), (0, LANES - GLA_GATE_RANK)))
    w_gate2 = jnp.pad(gla_w_gate2, ((0, 0), (0, LANES - GLA_GATE_RANK), (0, 0)))
    w = dict(
        norm_ffn_pre=row(norm_ffn_pre), norm_mix=row(norm_mix), norm_ffn_post=row(norm_ffn_post),
        norm_final=norm_final[None, :],
        ffn_w_gate=ffn_w_gate, ffn_w_up=ffn_w_up, ffn_w_down=ffn_w_down,
        conv_w_pw1=conv_w_pw1.astype(bf16), conv_b_pw1=row(conv_b_pw1),
        conv_w_dw=conv_w_dw, conv_b_dw=row(conv_b_dw), conv_ln_g=row(conv_ln_g), conv_ln_b=row(conv_ln_b),
        conv_w_pw2=conv_w_pw2.astype(bf16), conv_b_pw2=row(conv_b_pw2),
        gla_w_in=gla_w_in.astype(bf16), gla_w_z=w_z.astype(bf16), gla_w_gate2=w_gate2.astype(bf16),
        gla_b_gate=row(gla_b_gate), gla_onorm_g=row(gla_onorm_g), gla_w_out=gla_w_out.astype(bf16),
    )
    batch, seq, _ = x_prompt.shape
    dec_batch, steps, _ = x_sample.shape
    conv0 = jnp.zeros((state_conv.shape[0], batch) + state_conv.shape[2:], x_prompt.dtype)
    gla0 = jnp.zeros((state_gla.shape[0], batch) + state_gla.shape[2:], x_prompt.dtype)
    conv_in_s = state_conv.transpose(0, 2, 1, 3)
    xp = x_prompt.reshape(batch * seq, D_MODEL)
    xs = x_sample.transpose(1, 0, 2).reshape(steps * dec_batch, D_MODEL)

    wb = tuple(a[0, 0].astype(bf16) for a in (ffn_w_gate, ffn_w_up, ffn_w_down))
    conv_p = gla_p = conv_s = gla_s = None
    for i in range(DEPTH):
        xp, xs, wb = _ffn(xp, xs, wb, w, "norm_ffn_pre", i, (i, 1))
        xp, conv_p, gla_p = _mix_prompt(xp, conv0, gla0, w, i, conv_p, gla_p, batch, seq)
        xs, conv_s, gla_s = _mix_sample(xs, conv_in_s, state_gla, w, i, conv_s, gla_s, dec_batch, steps)
        last = i == DEPTH - 1
        xp, xs, wb = _ffn(xp, xs, wb, w, "norm_ffn_post", i, None if last else (i + 1, 0), final_norm=last)

    y_p = xp.reshape(batch, seq, D_MODEL)
    y_s = xs.reshape(steps, dec_batch, D_MODEL).transpose(1, 0, 2)
    return (y_p, y_s, conv_p, gla_p, conv_s.transpose(0, 2, 1, 3), gla_s)
```

```python
import functools

import jax
import jax.numpy as jnp
from jax import lax
from jax.experimental import pallas as pl
from jax.experimental.pallas import tpu as pltpu

f32 = jnp.float32
bf16 = jnp.bfloat16

D_MODEL = 1024
D_FF = 2816
DEPTH = 4
CONV_WIDTH = 31
CONV_STATE = CONV_WIDTH - 1
GLA_HEADS = 4
GLA_DK = 128
GLA_DV = 256
GLA_DKT = GLA_HEADS * GLA_DK
GLA_DVT = GLA_HEADS * GLA_DV
GLA_GATE_RANK = 16
GLA_TAU = 16.0
EPS = 1e-6

LANES = 128
SUBLANES = 8
N_SLABS = D_MODEL // LANES
MXU_COLS = 256
VMEM_LIMIT = 56 * 1024 * 1024

ROW_TILE = 512
FFN_TILE = 1024
FFN_CHUNK = 256
CONV_TILE = 512
CONV_HALO = 32
CONV_STRIDE = 4
CONV_BLOCK = SUBLANES * CONV_STRIDE
CONV_SAMPLE_BATCH = 8
GLA_CHUNK = 256
GLA_SEQS = 2
GLA_SUB = 32
GLA_SAMPLE_BATCH = 8


def _params(*semantics):
    return pltpu.CompilerParams(dimension_semantics=semantics, vmem_limit_bytes=VMEM_LIMIT)


def _resident(block, index=None):
    index = (0,) * len(block) if index is None else index
    return pl.BlockSpec(block, lambda *_: index, pipeline_mode=pl.Buffered(1))


def _vec(layer, width=D_MODEL, part=0):
    return _resident((None, 1, width), (layer, 0, part))


def _mat(layer, rows, cols, part=0):
    return _resident((None, rows, cols), (layer, 0, part))


def _rows(tile, width):
    return pl.BlockSpec((tile, width), lambda i: (i, 0))


def _stacked(prev, in_count, out_index):
    if prev is None:
        return [], [], {}
    return [pl.BlockSpec(memory_space=pl.ANY)], [prev], {in_count: out_index}


def _layer_block(first, n_layers, layer, block, index):
    lead, at = (n_layers, 0) if first else (None, layer)
    return pl.BlockSpec((lead,) + block, lambda *grid: (at,) + index(*grid))


def _own_layer(ref, layer, first):
    if not first:
        return ref
    for other in range(ref.shape[0]):
        if other != layer:
            ref[other] = jnp.zeros(ref.shape[1:], ref.dtype)
    return ref.at[layer]


def _rms(x, g):
    return x * lax.rsqrt(jnp.mean(x * x, axis=-1, keepdims=True) + EPS) * g


def _sigmoid(x):
    return 0.5 * jnp.tanh(0.5 * x) + 0.5


def _silu(x):
    return x * _sigmoid(x)


def _dot(a, b):
    return jnp.dot(a, b, preferred_element_type=f32)


def _dot_t_rhs(a, b):
    return lax.dot_general(a, b, (((1,), (1,)), ((), ())), preferred_element_type=f32)


def _dot_t_lhs(a, b):
    return lax.dot_general(a, b, (((0,), (0,)), ((), ())), preferred_element_type=f32)


def _ffn_rows(x, g_ref, wg_ref, wu_ref, wd_ref, gf_ref, a_scr, final_norm):
    rows = x.shape[0]
    h = _rms(x, g_ref[...]).astype(bf16)
    for j in range(D_FF // FFN_CHUNK):
        cols = pl.ds(j * FFN_CHUNK, FFN_CHUNK)
        gate = _dot(h, wg_ref[:, cols])
        up = _dot(h, wu_ref[:, cols])
        a_scr[0:rows, cols] = (_silu(gate) * up).astype(bf16)
    y = x + 0.5 * _dot(a_scr[0:rows, :], wd_ref[...])
    return _rms(y, gf_ref[...]) if final_norm else y


def _ffn_body(xp_ref, xs_ref, g_ref, wg_ref, wu_ref, wd_ref, gf_ref, *rest, final_norm, cast_next):
    op_ref, os_ref = rest[3:5] if cast_next else rest[0:2]
    a_scr = rest[-1]
    weights = (g_ref, wg_ref, wu_ref, wd_ref, gf_ref, a_scr, final_norm)
    op_ref[...] = _ffn_rows(xp_ref[...], *weights)

    @pl.when(pl.program_id(0) == pl.num_programs(0) - 1)
    def _():
        os_ref[...] = _ffn_rows(xs_ref[...], *weights)

    if cast_next:
        for src, dst in zip(rest[0:3], rest[5:8]):
            dst[...] = src[...].astype(bf16)


def _ffn(xp, xs, wb, w, gain, layer, nxt, *, final_norm=False):
    rows = xp.shape[0]
    tile = FFN_TILE
    steps = rows // tile
    cast_next = nxt is not None
    in_specs = [_rows(tile, D_MODEL), _resident(xs.shape), _vec(layer), _resident((D_MODEL, D_FF)),
                _resident((D_MODEL, D_FF)), _resident((D_FF, D_MODEL)), _resident((1, D_MODEL))]
    out_specs = [_rows(tile, D_MODEL), pl.BlockSpec(xs.shape, lambda i: (0, 0))]
    out_shape = [jax.ShapeDtypeStruct(xp.shape, f32), jax.ShapeDtypeStruct(xs.shape, f32)]
    args = [xp, xs, w[gain], *wb, w["norm_final"]]
    if cast_next:
        for name, (r, c) in (("ffn_w_gate", (D_MODEL, D_FF)), ("ffn_w_up", (D_MODEL, D_FF)),
                             ("ffn_w_down", (D_FF, D_MODEL))):
            in_specs.append(pl.BlockSpec((None, None, r // steps, c), lambda i: nxt + (i, 0)))
            out_specs.append(pl.BlockSpec((r // steps, c), lambda i: (i, 0)))
            out_shape.append(jax.ShapeDtypeStruct((r, c), bf16))
            args.append(w[name])
    outs = pl.pallas_call(
        functools.partial(_ffn_body, final_norm=final_norm, cast_next=cast_next),
        grid=(steps,),
        in_specs=in_specs,
        out_specs=out_specs,
        out_shape=out_shape,
        scratch_shapes=[pltpu.VMEM((tile, D_FF), bf16)],
        compiler_params=_params("arbitrary"),
        name="ffn",
    )(*args)
    return outs[0], outs[1], (tuple(outs[2:]) if cast_next else None)


def _conv_glu(x, g_ref, wa_ref, wb_ref, ba_ref, bb_ref):
    h = _rms(x, g_ref[...]).astype(bf16)
    a = _dot(h, wa_ref[...]) + ba_ref[...]
    gate = _dot(h, wb_ref[...]) + bb_ref[...]
    return a * _sigmoid(gate)


def _conv_a_body(x_ref, g_ref, wa_ref, wb_ref, ba_ref, bb_ref, u_ref):
    u_ref[...] = _conv_glu(x_ref[...], g_ref, wa_ref, wb_ref, ba_ref, bb_ref)


def _conv_a(x, w, layer, j):
    rows = x.shape[0]
    tile = min(ROW_TILE, rows)
    return pl.pallas_call(
        _conv_a_body,
        grid=(rows // tile,),
        in_specs=[_rows(tile, D_MODEL), _vec(layer), _mat(j, D_MODEL, D_MODEL, 0), _mat(j, D_MODEL, D_MODEL, 1),
                  _vec(j, part=0), _vec(j, part=1)],
        out_specs=_rows(tile, D_MODEL),
        out_shape=jax.ShapeDtypeStruct((rows, D_MODEL), f32),
        compiler_params=_params("arbitrary"),
        name="conv_glu",
    )(x, w["norm_mix"], w["conv_w_pw1"], w["conv_w_pw1"], w["conv_b_pw1"], w["conv_b_pw1"])


def _conv_prompt_body(x_ref, g_ref, wa_ref, wb_ref, ba_ref, bb_ref, st_ref, w_ref, b_ref,
                      lg_ref, lb_ref, w2_ref, b2_ref, *rest, layer, first):
    o_ref, ns_ref, hist, cout, taps = rest[-5:]
    t = pl.program_id(1)
    tile = x_ref.shape[0]

    @pl.when(t == 0)
    def _():
        for l in range(N_SLABS):
            lanes = slice(l * LANES, (l + 1) * LANES)
            hist[l, CONV_HALO - CONV_STATE:CONV_HALO, :] = st_ref[0, :, lanes]
            for j in range(CONV_WIDTH):
                taps[l, j] = jnp.broadcast_to(w_ref[j:j + 1, lanes], (2 * SUBLANES, LANES)).astype(bf16)

    @pl.when(t > 0)
    def _():
        for l in range(N_SLABS):
            hist[l, 0:CONV_HALO, :] = hist[l, tile:tile + CONV_HALO, :]

    x = x_ref[...]
    u = _conv_glu(x, g_ref, wa_ref, wb_ref, ba_ref, bb_ref)
    first_tap = CONV_HALO - CONV_STATE
    for l in range(N_SLABS):
        lanes = slice(l * LANES, (l + 1) * LANES)
        hist[l, CONV_HALO:CONV_HALO + tile, :] = u[:, lanes]

        def block_pair(pair, carry, l=l):
            bases = [(pair * 2 + half) * CONV_BLOCK for half in range(2)]
            acc = [None] * CONV_STRIDE
            for shift in range(first_tap, first_tap + CONV_WIDTH + CONV_STRIDE - 1):
                rows = jnp.concatenate(
                    [hist[l, pl.ds(base + shift, SUBLANES, stride=CONV_STRIDE), :] for base in bases],
                    axis=0).astype(bf16)
                for t0 in range(CONV_STRIDE):
                    j = shift - t0 - first_tap
                    if 0 <= j < CONV_WIDTH:
                        term = rows.astype(f32) * taps[l, j].astype(f32)
                        acc[t0] = term if acc[t0] is None else acc[t0] + term
            for t0 in range(CONV_STRIDE):
                for half, base in enumerate(bases):
                    cout[l, pl.ds(base + t0, SUBLANES, stride=CONV_STRIDE), :] = (
                        acc[t0][half * SUBLANES:(half + 1) * SUBLANES])
            return carry

        lax.fori_loop(0, tile // (2 * CONV_BLOCK), block_pair, 0)

    c = jnp.concatenate([cout[l] for l in range(N_SLABS)], axis=1) + b_ref[...]
    o_ref[...] = _conv_tail(c, x, lg_ref, lb_ref, w2_ref, b2_ref)
    _own_layer(ns_ref, layer, first)[0] = jnp.concatenate(
        [hist[l, tile + CONV_HALO - CONV_STATE:tile + CONV_HALO, :] for l in range(N_SLABS)], axis=1)


def _conv_prompt(x, states, w, layer, j, prev, batch, seq):
    tile = CONV_TILE
    nt = seq // tile
    n_layers = states.shape[0]
    extra_specs, extra_args, aliases = _stacked(prev, 13, 1)
    seq_tile = pl.BlockSpec((tile, D_MODEL), lambda i, t: (i * nt + t, 0))
    state_block = pl.BlockSpec((None, 1, CONV_STATE, D_MODEL), lambda i, t: (j, i, 0, 0))
    return pl.pallas_call(
        functools.partial(_conv_prompt_body, layer=j, first=prev is None),
        grid=(batch, nt),
        in_specs=[seq_tile, _vec(layer), _mat(j, D_MODEL, D_MODEL, 0), _mat(j, D_MODEL, D_MODEL, 1),
                  _vec(j, part=0), _vec(j, part=1), state_block, _mat(j, CONV_WIDTH, D_MODEL), _vec(j),
                  _vec(j), _vec(j), _mat(j, D_MODEL, D_MODEL), _vec(j)] + extra_specs,
        out_specs=[seq_tile, _layer_block(prev is None, n_layers, j, (1, CONV_STATE, D_MODEL),
                                          lambda i, t: (i, 0, 0))],
        out_shape=[jax.ShapeDtypeStruct((batch * seq, D_MODEL), f32),
                   jax.ShapeDtypeStruct((n_layers, batch, CONV_STATE, D_MODEL), f32)],
        scratch_shapes=[pltpu.VMEM((N_SLABS, CONV_HALO + tile, LANES), f32),
                        pltpu.VMEM((N_SLABS, tile, LANES), f32),
                        pltpu.VMEM((N_SLABS, CONV_WIDTH, 2 * SUBLANES, LANES), bf16)],
        input_output_aliases=aliases,
        compiler_params=_params("arbitrary", "arbitrary"),
        name="conv_prompt",
    )(x, w["norm_mix"], w["conv_w_pw1"], w["conv_w_pw1"], w["conv_b_pw1"], w["conv_b_pw1"], states,
      w["conv_w_dw"], w["conv_b_dw"], w["conv_ln_g"], w["conv_ln_b"], w["conv_w_pw2"], w["conv_b_pw2"],
      *extra_args)


def _conv_b_sample_body(u_ref, st_ref, w_ref, b_ref, *rest, layer, first):
    c_ref, ns_ref = rest[-2:]
    ns = _own_layer(ns_ref, layer, first)
    steps = u_ref.shape[0]
    acc = [None] * steps
    for i in range(CONV_STATE + steps):
        row = st_ref[i] if i < CONV_STATE else u_ref[i - CONV_STATE]
        for t in range(steps):
            j = i - t
            if 0 <= j < CONV_WIDTH and i <= CONV_STATE + t:
                term = row * w_ref[j:j + 1, :]
                acc[t] = term if acc[t] is None else acc[t] + term
        if i >= steps:
            ns[i - steps] = row
    for t in range(steps):
        c_ref[t] = acc[t] + b_ref[...]


def _conv_b_sample(u3, states, w, j, prev):
    steps, batch, _ = u3.shape
    bb = CONV_SAMPLE_BATCH
    n_layers = states.shape[0]
    first = prev is None
    extra_specs, extra_args, aliases = _stacked(prev, 4, 1)
    step_block = pl.BlockSpec((steps, bb, D_MODEL), lambda i: (0, i, 0))
    return pl.pallas_call(
        functools.partial(_conv_b_sample_body, layer=j, first=first),
        grid=(batch // bb,),
        in_specs=[step_block, pl.BlockSpec((None, CONV_STATE, bb, D_MODEL), lambda i: (j, 0, i, 0)),
                  _mat(j, CONV_WIDTH, D_MODEL), _vec(j)] + extra_specs,
        out_specs=[step_block,
                   _layer_block(first, n_layers, j, (CONV_STATE, bb, D_MODEL), lambda i: (0, i, 0))],
        out_shape=[jax.ShapeDtypeStruct((steps, batch, D_MODEL), f32),
                   jax.ShapeDtypeStruct((n_layers, CONV_STATE, batch, D_MODEL), f32)],
        input_output_aliases=aliases,
        compiler_params=_params("arbitrary"),
        name="conv_dw_sample",
    )(u3, states, w["conv_w_dw"], w["conv_b_dw"], *extra_args)


def _conv_tail(c, x, lg_ref, lb_ref, w2_ref, b2_ref):
    d = c - jnp.mean(c, axis=-1, keepdims=True)
    var = jnp.mean(d * d, axis=-1, keepdims=True)
    y = _silu(d * lax.rsqrt(var + EPS) * lg_ref[...] + lb_ref[...])
    return x + _dot(y.astype(bf16), w2_ref[...]) + b2_ref[...]


def _conv_c_body(c_ref, x_ref, lg_ref, lb_ref, w2_ref, b2_ref, o_ref):
    o_ref[...] = _conv_tail(c_ref[...], x_ref[...], lg_ref, lb_ref, w2_ref, b2_ref)


def _conv_c(c, x, w, j):
    rows = x.shape[0]
    tile = min(ROW_TILE, rows)
    return pl.pallas_call(
        _conv_c_body,
        grid=(rows // tile,),
        in_specs=[_rows(tile, D_MODEL), _rows(tile, D_MODEL), _vec(j), _vec(j), _mat(j, D_MODEL, D_MODEL), _vec(j)],
        out_specs=_rows(tile, D_MODEL),
        out_shape=jax.ShapeDtypeStruct((rows, D_MODEL), f32),
        compiler_params=_params("arbitrary"),
        name="conv_out",
    )(c, x, w["conv_ln_g"], w["conv_ln_b"], w["conv_w_pw2"], w["conv_b_pw2"])


def _gla_a_body(x_ref, g_ref, wq_ref, wk_ref, wv_ref, wr_ref, wz_ref, wg2_ref, bg_ref,
                q_ref, k_ref, v_ref, r_ref, la_ref):
    q, k, v, r, la = _gla_project(x_ref[...], g_ref, wq_ref, wk_ref, wv_ref, wr_ref, wz_ref, wg2_ref, bg_ref)
    q_ref[...] = q
    k_ref[...] = k
    v_ref[...] = v
    r_ref[...] = r
    la_ref[...] = la


def _gla_project(x, g_ref, wq_ref, wk_ref, wv_ref, wr_ref, wz_ref, wg2_ref, bg_ref):
    h = _rms(x, g_ref[...]).astype(bf16)
    q = _dot(h, wq_ref[...]) * (GLA_DK ** -0.5)
    k = _dot(h, wk_ref[...])
    v = _dot(h, wv_ref[...])
    r = _dot(h, wr_ref[...])
    z = _dot(h, wz_ref[...])
    pre = _dot(z.astype(bf16), wg2_ref[...]) + bg_ref[...]
    la = (jnp.minimum(pre, 0.0) - jnp.log1p(jnp.exp(-jnp.abs(pre)))) * (1.0 / GLA_TAU)
    return q, k, v, r, la


def _gla_a(x, w, layer, j):
    rows = x.shape[0]
    tile = min(ROW_TILE, rows)
    widths = (GLA_DKT, GLA_DKT, GLA_DVT, GLA_DVT, GLA_DKT)
    return pl.pallas_call(
        _gla_a_body,
        grid=(rows // tile,),
        in_specs=[_rows(tile, D_MODEL), _vec(layer),
                  _mat(j, D_MODEL, GLA_DKT, 0), _mat(j, D_MODEL, GLA_DKT, 1),
                  _mat(j, D_MODEL, GLA_DVT, 1), _mat(j, D_MODEL, GLA_DVT, 2),
                  _mat(j, D_MODEL, LANES), _mat(j, LANES, GLA_DKT), _vec(j, GLA_DKT)],
        out_specs=[_rows(tile, n) for n in widths],
        out_shape=[jax.ShapeDtypeStruct((rows, n), f32) for n in widths],
        compiler_params=_params("arbitrary"),
        name="gla_proj",
    )(x, w["norm_mix"], w["gla_w_in"], w["gla_w_in"], w["gla_w_in"], w["gla_w_in"], w["gla_w_z"],
      w["gla_w_gate2"], w["gla_b_gate"])


def _split_bf16(x):
    hi = x.astype(bf16)
    return hi, (x - hi.astype(f32)).astype(bf16)


def _decay_columns(la_hi, la_lo, ones):
    total = _dot_t_lhs(la_hi, ones) + _dot_t_lhs(la_lo, ones)
    return jnp.concatenate([jnp.exp(total)] * (GLA_DV // LANES), axis=1)


def _gla_prompt_body(x_ref, g_ref, wq_ref, wk_ref, wv_ref, wr_ref, wz_ref, wg2_ref, bg_ref, s0_ref,
                     go_ref, wo_ref, *rest, layer, first):
    out_ref, so_ref, state, scores = rest[-4:]
    c = pl.program_id(1)
    seqs, chunk, _ = x_ref.shape

    @pl.when(c == 0)
    def _():
        state[...] = s0_ref[...]

    tri = (lax.broadcasted_iota(jnp.int32, (chunk, chunk), 1)
           <= lax.broadcasted_iota(jnp.int32, (chunk, chunk), 0)).astype(bf16)
    ones = jnp.ones((chunk, LANES), bf16)
    rows = slice(0, chunk)

    def projection(s):
        x = x_ref[s]
        h = _rms(x, g_ref[...]).astype(bf16)
        got = {"x": x}
        parts = {"q": [], "k": [], "v": [], "r": []}

        def gate():
            z = _dot(h, wz_ref[...])
            pre = _dot(z.astype(bf16), wg2_ref[...]) + bg_ref[...]
            got["la"] = (jnp.minimum(pre, 0.0) - jnp.log1p(jnp.exp(-jnp.abs(pre)))) * (1.0 / GLA_TAU)

        def column_tile(name, w_ref, c0, width, scale):
            def piece():
                out = _dot(h, w_ref[:, c0:c0 + MXU_COLS])
                parts[name].append(out * scale if scale is not None else out)
                if c0 + MXU_COLS == width:
                    got[name] = jnp.concatenate(parts[name], axis=1)
            return piece

        pieces = [gate]
        for name, w_ref, width, scale in (("q", wq_ref, GLA_DKT, GLA_DK ** -0.5), ("k", wk_ref, GLA_DKT, None),
                                          ("v", wv_ref, GLA_DVT, None), ("r", wr_ref, GLA_DVT, None)):
            pieces += [column_tile(name, w_ref, c0, width, scale) for c0 in range(0, width, MXU_COLS)]
        return got, pieces

    cur, pieces = projection(0)
    for piece in pieces:
        piece()
    for s in range(seqs):
        nxt, pieces = projection(s + 1) if s + 1 < seqs else (None, [])
        la_hi, la_lo = _split_bf16(cur["la"])
        gated = _gla_chunk(rows, cur["q"], cur["k"], cur["v"], cur["r"], la_hi, la_lo, tri, ones,
                           state.at[s], scores.at[s], go_ref, pieces)
        out_ref[s] = cur["x"] + _dot(gated, wo_ref[...])
        cur = nxt

    @pl.when(c == pl.num_programs(1) - 1)
    def _():
        _own_layer(so_ref, layer, first)[...] = state[...]


def _gla_chunk(rows, q, k, v, r, la_hi, la_lo, tri, ones, state, scores, go_ref, between=()):
    chunk = tri.shape[0]
    blocks = chunk // GLA_SUB
    slots = GLA_HEADS * blocks
    emit_at = {}
    for n, piece in enumerate(between):
        emit_at.setdefault(n * slots // len(between), []).append(piece)
    la_hi, la_lo = la_hi[rows], la_lo[rows]
    b_all = _dot(tri, la_hi) + _dot(tri, la_lo)
    gated = []
    for h in range(GLA_HEADS):
        ks = slice(h * GLA_DK, (h + 1) * GLA_DK)
        vs = slice(h * GLA_DV, (h + 1) * GLA_DV)
        b = b_all[:, ks]
        qh = q[rows, ks]
        kh = k[rows, ks]
        vb = v[rows, vs].astype(bf16)
        b_last = b[chunk - 1:chunk, :]
        q_dec = (qh * jnp.exp(b)).astype(bf16)
        k_end = (kh * jnp.exp(b_last - b)).astype(bf16)
        s_old = state[h]
        o_inter = _dot(q_dec, s_old.astype(bf16))

        for i in range(chunk // GLA_SUB):
            r0 = i * GLA_SUB
            nk = r0 + GLA_SUB
            npad = LANES * (-(-nk // LANES))
            bq = b[r0:nk]
            bk = b[0:nk]
            if i == 0:
                q_exp, k_exp = bq, -bk
            else:
                ref = b[r0 - 1:r0, :]
                q_exp, k_exp = bq - ref, ref - bk
            qi = (qh[r0:nk] * jnp.exp(q_exp)).astype(bf16)
            ki = (kh[0:nk] * jnp.exp(k_exp)).astype(bf16)
            if nk < npad:
                ki = jnp.concatenate([ki, jnp.zeros((npad - nk, GLA_DK), bf16)], axis=0)
            s = _dot_t_rhs(qi, ki)
            causal = (lax.broadcasted_iota(jnp.int32, (GLA_SUB, npad), 1)
                      <= lax.broadcasted_iota(jnp.int32, (GLA_SUB, npad), 0) + r0)
            scores[h, r0:nk, 0:npad] = jnp.where(causal, s, 0.0).astype(bf16)
            if npad < chunk:
                scores[h, r0:nk, npad:chunk] = jnp.zeros((GLA_SUB, chunk - npad), bf16)
            for piece in emit_at.get(h * blocks + i, ()):
                piece()

        o = o_inter + _dot(scores[h], vb)
        gated.append(_gla_gate(o, r[rows, vs], go_ref))
        state[h] = s_old * _decay_columns(la_hi[:, ks], la_lo[:, ks], ones) + _dot_t_lhs(k_end, vb)
    return jnp.concatenate(gated, axis=1)


def _gla_prompt(x, states, w, layer, j, prev, batch, seq):
    chunk, seqs = GLA_CHUNK, GLA_SEQS
    n_layers = states.shape[0]
    first = prev is None
    extra_specs, extra_args, aliases = _stacked(prev, 12, 1)
    seq_tile = pl.BlockSpec((seqs, chunk, D_MODEL), lambda i, c: (i, c, 0))
    state_shape = (seqs, GLA_HEADS, GLA_DK, GLA_DV)
    out, new_states = pl.pallas_call(
        functools.partial(_gla_prompt_body, layer=j, first=first),
        grid=(batch // seqs, seq // chunk),
        in_specs=[seq_tile, _vec(layer),
                  _mat(j, D_MODEL, GLA_DKT, 0), _mat(j, D_MODEL, GLA_DKT, 1),
                  _mat(j, D_MODEL, GLA_DVT, 1), _mat(j, D_MODEL, GLA_DVT, 2),
                  _mat(j, D_MODEL, LANES), _mat(j, LANES, GLA_DKT), _vec(j, GLA_DKT),
                  pl.BlockSpec((None,) + state_shape, lambda i, c: (j, i, 0, 0, 0)),
                  _vec(j, GLA_DV), _mat(j, GLA_DVT, D_MODEL)] + extra_specs,
        out_specs=[seq_tile, _layer_block(first, n_layers, j, state_shape, lambda i, c: (i, 0, 0, 0))],
        out_shape=[jax.ShapeDtypeStruct((batch, seq, D_MODEL), f32),
                   jax.ShapeDtypeStruct((n_layers, batch, GLA_HEADS, GLA_DK, GLA_DV), f32)],
        scratch_shapes=[pltpu.VMEM(state_shape, f32), pltpu.VMEM((seqs, GLA_HEADS, chunk, chunk), bf16)],
        input_output_aliases=aliases,
        compiler_params=_params("arbitrary", "arbitrary"),
        name="gla_prompt",
    )(x.reshape(batch, seq, D_MODEL), w["norm_mix"], w["gla_w_in"], w["gla_w_in"], w["gla_w_in"], w["gla_w_in"],
      w["gla_w_z"], w["gla_w_gate2"], w["gla_b_gate"], states, w["gla_onorm_g"], w["gla_w_out"], *extra_args)
    return out.reshape(batch * seq, D_MODEL), new_states


def _gla_b_sample_body(q_ref, k_ref, v_ref, la_ref, s0_ref, *rest, layer, first):
    o_ref, so_ref = rest[-2:]
    so = _own_layer(so_ref, layer, first)
    steps, bb, _ = q_ref.shape
    causal = (lax.broadcasted_iota(jnp.int32, (steps, steps), 1)
              <= lax.broadcasted_iota(jnp.int32, (steps, steps), 0))
    ones = jnp.ones((steps, LANES), bf16)

    for i in range(bb):
        la = la_ref[:, i, :]
        q, k, v = q_ref[:, i, :], k_ref[:, i, :], v_ref[:, i, :]
        cum = [la[0:1]]
        for t in range(1, steps):
            cum.append(cum[-1] + la[t:t + 1])
        b_all = jnp.concatenate(cum, axis=0)
        la_hi, la_lo = _split_bf16(la)
        heads = []
        for h in range(GLA_HEADS):
            ks = slice(h * GLA_DK, (h + 1) * GLA_DK)
            vs = slice(h * GLA_DV, (h + 1) * GLA_DV)
            b = b_all[:, ks]
            qh = q[:, ks]
            kh = k[:, ks]
            vb = v[:, vs].astype(bf16)
            b_last = b[steps - 1:steps, :]
            q_dec = (qh * jnp.exp(b)).astype(bf16)
            k_inv = (kh * jnp.exp(-b)).astype(bf16)
            k_end = (kh * jnp.exp(b_last - b)).astype(bf16)
            s_old = s0_ref[i, h]
            sc = jnp.where(causal, _dot_t_rhs(q_dec, k_inv), 0.0).astype(bf16)
            heads.append(_dot(sc, vb) + _dot(q_dec, s_old.astype(bf16)))
            so[i, h] = s_old * _decay_columns(la_hi[:, ks], la_lo[:, ks], ones) + _dot_t_lhs(k_end, vb)
        o_ref[:, i, :] = jnp.concatenate(heads, axis=1)


def _gla_b_sample(q, k, v, la, states, j, prev):
    steps, batch, _ = q.shape
    bb = GLA_SAMPLE_BATCH
    n_layers = states.shape[0]
    first = prev is None
    extra_specs, extra_args, aliases = _stacked(prev, 5, 1)
    step_block = lambda n: pl.BlockSpec((steps, bb, n), lambda i: (0, i, 0))
    state_shape = (bb, GLA_HEADS, GLA_DK, GLA_DV)
    return pl.pallas_call(
        functools.partial(_gla_b_sample_body, layer=j, first=first),
        grid=(batch // bb,),
        in_specs=[step_block(GLA_DKT), step_block(GLA_DKT), step_block(GLA_DVT), step_block(GLA_DKT),
                  pl.BlockSpec((None,) + state_shape, lambda i: (j, i, 0, 0, 0))] + extra_specs,
        out_specs=[step_block(GLA_DVT),
                   _layer_block(first, n_layers, j, state_shape, lambda i: (i, 0, 0, 0))],
        out_shape=[jax.ShapeDtypeStruct((steps, batch, GLA_DVT), f32),
                   jax.ShapeDtypeStruct((n_layers, batch, GLA_HEADS, GLA_DK, GLA_DV), f32)],
        input_output_aliases=aliases,
        compiler_params=_params("arbitrary"),
        name="gla_core_sample",
    )(q, k, v, la, states, *extra_args)


def _gla_gate(o, r, go_ref):
    return (_rms(o, go_ref[...]) * _silu(r)).astype(bf16)


def _gla_c_body(o_ref, r_ref, x_ref, go_ref, wo_ref, out_ref):
    parts = []
    for h in range(GLA_HEADS):
        vs = slice(h * GLA_DV, (h + 1) * GLA_DV)
        parts.append(_gla_gate(o_ref[:, vs], r_ref[:, vs], go_ref))
    out_ref[...] = x_ref[...] + _dot(jnp.concatenate(parts, axis=1), wo_ref[...])


def _gla_c(o, r, x, w, j):
    rows = x.shape[0]
    tile = min(ROW_TILE, rows)
    return pl.pallas_call(
        _gla_c_body,
        grid=(rows // tile,),
        in_specs=[_rows(tile, GLA_DVT), _rows(tile, GLA_DVT), _rows(tile, D_MODEL), _vec(j, GLA_DV),
                  _mat(j, GLA_DVT, D_MODEL)],
        out_specs=_rows(tile, D_MODEL),
        out_shape=jax.ShapeDtypeStruct((rows, D_MODEL), f32),
        compiler_params=_params("arbitrary"),
        name="gla_out",
    )(o, r, x, w["gla_onorm_g"], w["gla_w_out"])


def _mix_sample(x, conv_state, gla_state, w, i, new_conv, new_gla, batch, steps):
    j = i // 2
    if i % 2 == 0:
        u = _conv_a(x, w, i, j)
        c, new_conv = _conv_b_sample(u.reshape(steps, batch, D_MODEL), conv_state, w, j, new_conv)
        x = _conv_c(c.reshape(steps * batch, D_MODEL), x, w, j)
    else:
        q, k, v, r, la = _gla_a(x, w, i, j)
        by_step = lambda a: a.reshape(steps, batch, a.shape[-1])
        o, new_gla = _gla_b_sample(by_step(q), by_step(k), by_step(v), by_step(la), gla_state, j, new_gla)
        x = _gla_c(o.reshape(steps * batch, GLA_DVT), r, x, w, j)
    return x, new_conv, new_gla


def _mix_prompt(x, conv_state, gla_state, w, i, new_conv, new_gla, batch, seq):
    j = i // 2
    if i % 2 == 0:
        x, new_conv = _conv_prompt(x, conv_state, w, i, j, new_conv, batch, seq)
    else:
        x, new_gla = _gla_prompt(x, gla_state, w, i, j, new_gla, batch, seq)
    return x, new_conv, new_gla


def kernel(x_prompt, x_sample, state_conv, state_gla, norm_ffn_pre, norm_mix, norm_ffn_post, norm_final,
           ffn_w_gate, ffn_w_up, ffn_w_down, conv_w_pw1, conv_b_pw1, conv_w_dw, conv_b_dw, conv_ln_g, conv_ln_b,
           conv_w_pw2, conv_b_pw2, gla_w_in, gla_w_gate2, gla_b_gate, gla_onorm_g, gla_w_out):
    row = lambda a: a[..., None, :]
    z_start = 2 * GLA_DKT + 2 * GLA_DVT
    w_z = jnp.pad(gla_w_in[:, :, z_start:], ((0, 0), (0, 0), (0, LANES - GLA_GATE_RANK)))
    w_gate2 = jnp.pad(gla_w_gate2, ((0, 0), (0, LANES - GLA_GATE_RANK), (0, 0)))
    w = dict(
        norm_ffn_pre=row(norm_ffn_pre), norm_mix=row(norm_mix), norm_ffn_post=row(norm_ffn_post),
        norm_final=norm_final[None, :],
        ffn_w_gate=ffn_w_gate, ffn_w_up=ffn_w_up, ffn_w_down=ffn_w_down,
        conv_w_pw1=conv_w_pw1.astype(bf16), conv_b_pw1=row(conv_b_pw1),
        conv_w_dw=conv_w_dw, conv_b_dw=row(conv_b_dw), conv_ln_g=row(conv_ln_g), conv_ln_b=row(conv_ln_b),
        conv_w_pw2=conv_w_pw2.astype(bf16), conv_b_pw2=row(conv_b_pw2),
        gla_w_in=gla_w_in.astype(bf16), gla_w_z=w_z.astype(bf16), gla_w_gate2=w_gate2.astype(bf16),
        gla_b_gate=row(gla_b_gate), gla_onorm_g=row(gla_onorm_g), gla_w_out=gla_w_out.astype(bf16),
    )
    batch, seq, _ = x_prompt.shape
    dec_batch, steps, _ = x_sample.shape
    conv0 = jnp.zeros((state_conv.shape[0], batch) + state_conv.shape[2:], x_prompt.dtype)
    gla0 = jnp.zeros((state_gla.shape[0], batch) + state_gla.shape[2:], x_prompt.dtype)
    conv_in_s = state_conv.transpose(0, 2, 1, 3)
    xp = x_prompt.reshape(batch * seq, D_MODEL)
    xs = x_sample.transpose(1, 0, 2).reshape(steps * dec_batch, D_MODEL)

    wb = tuple(a[0, 0].astype(bf16) for a in (ffn_w_gate, ffn_w_up, ffn_w_down))
    conv_p = gla_p = conv_s = gla_s = None
    for i in range(DEPTH):
        xp, xs, wb = _ffn(xp, xs, wb, w, "norm_ffn_pre", i, (i, 1))
        xp, conv_p, gla_p = _mix_prompt(xp, conv0, gla0, w, i, conv_p, gla_p, batch, seq)
        xs, conv_s, gla_s = _mix_sample(xs, conv_in_s, state_gla, w, i, conv_s, gla_s, dec_batch, steps)
        last = i == DEPTH - 1
        xp, xs, wb = _ffn(xp, xs, wb, w, "norm_ffn_post", i, None if last else (i + 1, 0), final_norm=last)

    y_p = xp.reshape(batch, seq, D_MODEL)
    y_s = xs.reshape(steps, dec_batch, D_MODEL).transpose(1, 0, 2)
    return (y_p, y_s, conv_p, gla_p, conv_s.transpose(0, 2, 1, 3), gla_s)
```

```python
import functools

import jax
import jax.numpy as jnp
from jax import lax
from jax.experimental import pallas as pl
from jax.experimental.pallas import tpu as pltpu

f32 = jnp.float32
bf16 = jnp.bfloat16

D_MODEL = 1024
D_FF = 2816
DEPTH = 4
CONV_WIDTH = 31
CONV_STATE = CONV_WIDTH - 1
GLA_HEADS = 4
GLA_DK = 128
GLA_DV = 256
GLA_DKT = GLA_HEADS * GLA_DK
GLA_DVT = GLA_HEADS * GLA_DV
GLA_GATE_RANK = 16
GLA_TAU = 16.0
EPS = 1e-6

LANES = 128
SUBLANES = 8
N_SLABS = D_MODEL // LANES
MXU_COLS = 256
VMEM_LIMIT = 56 * 1024 * 1024

ROW_TILE = 512
FFN_TILE = 1024
FFN_CHUNK = 256
CONV_TILE = 512
CONV_HALO = 32
CONV_STRIDE = 4
CONV_BLOCK = SUBLANES * CONV_STRIDE
CONV_SAMPLE_BATCH = 8
GLA_CHUNK = 256
GLA_SEQS = 2
GLA_SUB = 32
GLA_SAMPLE_BATCH = 8


def _params(*semantics):
    return pltpu.CompilerParams(dimension_semantics=semantics, vmem_limit_bytes=VMEM_LIMIT)


def _resident(block, index=None):
    index = (0,) * len(block) if index is None else index
    return pl.BlockSpec(block, lambda *_: index, pipeline_mode=pl.Buffered(1))


def _vec(layer, width=D_MODEL, part=0):
    return _resident((None, 1, width), (layer, 0, part))


def _mat(layer, rows, cols, part=0):
    return _resident((None, rows, cols), (layer, 0, part))


def _rows(tile, width):
    return pl.BlockSpec((tile, width), lambda i: (i, 0))


def _stacked(prev, in_count, out_index):
    if prev is None:
        return [], [], {}
    return [pl.BlockSpec(memory_space=pl.ANY)], [prev], {in_count: out_index}


def _layer_block(first, n_layers, layer, block, index):
    lead, at = (n_layers, 0) if first else (None, layer)
    return pl.BlockSpec((lead,) + block, lambda *grid: (at,) + index(*grid))


def _own_layer(ref, layer, first):
    if not first:
        return ref
    for other in range(ref.shape[0]):
        if other != layer:
            ref[other] = jnp.zeros(ref.shape[1:], ref.dtype)
    return ref.at[layer]


def _rms(x, g):
    return x * lax.rsqrt(jnp.mean(x * x, axis=-1, keepdims=True) + EPS) * g


def _sigmoid(x):
    return 0.5 * jnp.tanh(0.5 * x) + 0.5


def _silu(x):
    return x * _sigmoid(x)


def _dot(a, b):
    return jnp.dot(a, b, preferred_element_type=f32)


def _dot_t_rhs(a, b):
    return lax.dot_general(a, b, (((1,), (1,)), ((), ())), preferred_element_type=f32)


def _dot_t_lhs(a, b):
    return lax.dot_general(a, b, (((0,), (0,)), ((), ())), preferred_element_type=f32)


def _ffn_rows(x, g_ref, wg_ref, wu_ref, wd_ref, gf_ref, a_scr, final_norm):
    rows = x.shape[0]
    h = _rms(x, g_ref[...]).astype(bf16)
    for j in range(D_FF // FFN_CHUNK):
        cols = pl.ds(j * FFN_CHUNK, FFN_CHUNK)
        gate = _dot(h, wg_ref[:, cols])
        up = _dot(h, wu_ref[:, cols])
        a_scr[0:rows, cols] = (_silu(gate) * up).astype(bf16)
    y = x + 0.5 * _dot(a_scr[0:rows, :], wd_ref[...])
    return _rms(y, gf_ref[...]) if final_norm else y


def _ffn_tile(x_ref, o_ref, g_ref, wg_ref, wu_ref, wd_ref, gf_ref, a_scr, final_norm):
    half = x_ref.shape[0] // 2
    lo, hi = slice(0, half), slice(half, 2 * half)
    chunks = [pl.ds(j * FFN_CHUNK, FFN_CHUNK) for j in range(D_FF // FFN_CHUNK)]

    def hidden(h, rows, cols):
        a_scr[rows, cols] = (_silu(_dot(h, wg_ref[:, cols])) * _dot(h, wu_ref[:, cols])).astype(bf16)

    def finish(x, rows):
        y = x + 0.5 * _dot(a_scr[rows, :], wd_ref[...])
        o_ref[rows, :] = _rms(y, gf_ref[...]) if final_norm else y

    x_lo = x_ref[lo, :]
    h_lo = _rms(x_lo, g_ref[...]).astype(bf16)
    hidden(h_lo, lo, chunks[0])
    x_hi = x_ref[hi, :]
    h_hi = _rms(x_hi, g_ref[...]).astype(bf16)
    for cols in chunks[1:]:
        hidden(h_lo, lo, cols)
    hidden(h_hi, hi, chunks[0])
    finish(x_lo, lo)
    for cols in chunks[1:]:
        hidden(h_hi, hi, cols)
    finish(x_hi, hi)


def _ffn_body(xp_ref, xs_ref, g_ref, wg_ref, wu_ref, wd_ref, gf_ref, *rest, final_norm, cast_next):
    op_ref, os_ref = rest[3:5] if cast_next else rest[0:2]
    a_scr = rest[-1]
    weights = (g_ref, wg_ref, wu_ref, wd_ref, gf_ref, a_scr, final_norm)
    _ffn_tile(xp_ref, op_ref, *weights)

    @pl.when(pl.program_id(0) == pl.num_programs(0) - 1)
    def _():
        os_ref[...] = _ffn_rows(xs_ref[...], *weights)

    if cast_next:
        for src, dst in zip(rest[0:3], rest[5:8]):
            dst[...] = src[...].astype(bf16)


def _ffn(xp, xs, wb, w, gain, layer, nxt, *, final_norm=False):
    rows = xp.shape[0]
    tile = FFN_TILE
    steps = rows // tile
    cast_next = nxt is not None
    in_specs = [_rows(tile, D_MODEL), _resident(xs.shape), _vec(layer), _resident((D_MODEL, D_FF)),
                _resident((D_MODEL, D_FF)), _resident((D_FF, D_MODEL)), _resident((1, D_MODEL))]
    out_specs = [_rows(tile, D_MODEL), pl.BlockSpec(xs.shape, lambda i: (0, 0))]
    out_shape = [jax.ShapeDtypeStruct(xp.shape, f32), jax.ShapeDtypeStruct(xs.shape, f32)]
    args = [xp, xs, w[gain], *wb, w["norm_final"]]
    if cast_next:
        for name, (r, c) in (("ffn_w_gate", (D_MODEL, D_FF)), ("ffn_w_up", (D_MODEL, D_FF)),
                             ("ffn_w_down", (D_FF, D_MODEL))):
            in_specs.append(pl.BlockSpec((None, None, r // steps, c), lambda i: nxt + (i, 0)))
            out_specs.append(pl.BlockSpec((r // steps, c), lambda i: (i, 0)))
            out_shape.append(jax.ShapeDtypeStruct((r, c), bf16))
            args.append(w[name])
    outs = pl.pallas_call(
        functools.partial(_ffn_body, final_norm=final_norm, cast_next=cast_next),
        grid=(steps,),
        in_specs=in_specs,
        out_specs=out_specs,
        out_shape=out_shape,
        scratch_shapes=[pltpu.VMEM((tile, D_FF), bf16)],
        compiler_params=_params("arbitrary"),
        name="ffn",
    )(*args)
    return outs[0], outs[1], (tuple(outs[2:]) if cast_next else None)


def _conv_glu(x, g_ref, wa_ref, wb_ref, ba_ref, bb_ref):
    h = _rms(x, g_ref[...]).astype(bf16)
    a = _dot(h, wa_ref[...]) + ba_ref[...]
    gate = _dot(h, wb_ref[...]) + bb_ref[...]
    return a * _sigmoid(gate)


def _conv_a_body(x_ref, g_ref, wa_ref, wb_ref, ba_ref, bb_ref, u_ref):
    u_ref[...] = _conv_glu(x_ref[...], g_ref, wa_ref, wb_ref, ba_ref, bb_ref)


def _conv_a(x, w, layer, j):
    rows = x.shape[0]
    tile = min(ROW_TILE, rows)
    return pl.pallas_call(
        _conv_a_body,
        grid=(rows // tile,),
        in_specs=[_rows(tile, D_MODEL), _vec(layer), _mat(j, D_MODEL, D_MODEL, 0), _mat(j, D_MODEL, D_MODEL, 1),
                  _vec(j, part=0), _vec(j, part=1)],
        out_specs=_rows(tile, D_MODEL),
        out_shape=jax.ShapeDtypeStruct((rows, D_MODEL), f32),
        compiler_params=_params("arbitrary"),
        name="conv_glu",
    )(x, w["norm_mix"], w["conv_w_pw1"], w["conv_w_pw1"], w["conv_b_pw1"], w["conv_b_pw1"])


def _conv_prompt_body(x_ref, g_ref, wa_ref, wb_ref, ba_ref, bb_ref, st_ref, w_ref, b_ref,
                      lg_ref, lb_ref, w2_ref, b2_ref, *rest, layer, first):
    o_ref, ns_ref, hist, cout, taps = rest[-5:]
    t = pl.program_id(1)
    tile = x_ref.shape[0]

    @pl.when(t == 0)
    def _():
        for l in range(N_SLABS):
            lanes = slice(l * LANES, (l + 1) * LANES)
            hist[l, CONV_HALO - CONV_STATE:CONV_HALO, :] = st_ref[0, :, lanes]
            for j in range(CONV_WIDTH):
                taps[l, j] = jnp.broadcast_to(w_ref[j:j + 1, lanes], (2 * SUBLANES, LANES)).astype(bf16)

    @pl.when(t > 0)
    def _():
        for l in range(N_SLABS):
            hist[l, 0:CONV_HALO, :] = hist[l, tile:tile + CONV_HALO, :]

    x = x_ref[...]
    u = _conv_glu(x, g_ref, wa_ref, wb_ref, ba_ref, bb_ref)
    first_tap = CONV_HALO - CONV_STATE
    for l in range(N_SLABS):
        lanes = slice(l * LANES, (l + 1) * LANES)
        hist[l, CONV_HALO:CONV_HALO + tile, :] = u[:, lanes]

        def block_pair(pair, carry, l=l):
            bases = [(pair * 2 + half) * CONV_BLOCK for half in range(2)]
            acc = [None] * CONV_STRIDE
            for shift in range(first_tap, first_tap + CONV_WIDTH + CONV_STRIDE - 1):
                rows = jnp.concatenate(
                    [hist[l, pl.ds(base + shift, SUBLANES, stride=CONV_STRIDE), :] for base in bases],
                    axis=0).astype(bf16)
                for t0 in range(CONV_STRIDE):
                    j = shift - t0 - first_tap
                    if 0 <= j < CONV_WIDTH:
                        term = rows.astype(f32) * taps[l, j].astype(f32)
                        acc[t0] = term if acc[t0] is None else acc[t0] + term
            for t0 in range(CONV_STRIDE):
                for half, base in enumerate(bases):
                    cout[l, pl.ds(base + t0, SUBLANES, stride=CONV_STRIDE), :] = (
                        acc[t0][half * SUBLANES:(half + 1) * SUBLANES])
            return carry

        lax.fori_loop(0, tile // (2 * CONV_BLOCK), block_pair, 0)

    c = jnp.concatenate([cout[l] for l in range(N_SLABS)], axis=1) + b_ref[...]
    o_ref[...] = _conv_tail(c, x, lg_ref, lb_ref, w2_ref, b2_ref)
    _own_layer(ns_ref, layer, first)[0] = jnp.concatenate(
        [hist[l, tile + CONV_HALO - CONV_STATE:tile + CONV_HALO, :] for l in range(N_SLABS)], axis=1)


def _conv_prompt(x, states, w, layer, j, prev, batch, seq):
    tile = CONV_TILE
    nt = seq // tile
    n_layers = states.shape[0]
    extra_specs, extra_args, aliases = _stacked(prev, 13, 1)
    seq_tile = pl.BlockSpec((tile, D_MODEL), lambda i, t: (i * nt + t, 0))
    state_block = pl.BlockSpec((None, 1, CONV_STATE, D_MODEL), lambda i, t: (j, i, 0, 0))
    return pl.pallas_call(
        functools.partial(_conv_prompt_body, layer=j, first=prev is None),
        grid=(batch, nt),
        in_specs=[seq_tile, _vec(layer), _mat(j, D_MODEL, D_MODEL, 0), _mat(j, D_MODEL, D_MODEL, 1),
                  _vec(j, part=0), _vec(j, part=1), state_block, _mat(j, CONV_WIDTH, D_MODEL), _vec(j),
                  _vec(j), _vec(j), _mat(j, D_MODEL, D_MODEL), _vec(j)] + extra_specs,
        out_specs=[seq_tile, _layer_block(prev is None, n_layers, j, (1, CONV_STATE, D_MODEL),
                                          lambda i, t: (i, 0, 0))],
        out_shape=[jax.ShapeDtypeStruct((batch * seq, D_MODEL), f32),
                   jax.ShapeDtypeStruct((n_layers, batch, CONV_STATE, D_MODEL), f32)],
        scratch_shapes=[pltpu.VMEM((N_SLABS, CONV_HALO + tile, LANES), f32),
                        pltpu.VMEM((N_SLABS, tile, LANES), f32),
                        pltpu.VMEM((N_SLABS, CONV_WIDTH, 2 * SUBLANES, LANES), bf16)],
        input_output_aliases=aliases,
        compiler_params=_params("arbitrary", "arbitrary"),
        name="conv_prompt",
    )(x, w["norm_mix"], w["conv_w_pw1"], w["conv_w_pw1"], w["conv_b_pw1"], w["conv_b_pw1"], states,
      w["conv_w_dw"], w["conv_b_dw"], w["conv_ln_g"], w["conv_ln_b"], w["conv_w_pw2"], w["conv_b_pw2"],
      *extra_args)


def _conv_b_sample_body(u_ref, st_ref, w_ref, b_ref, *rest, layer, first):
    c_ref, ns_ref = rest[-2:]
    ns = _own_layer(ns_ref, layer, first)
    steps = u_ref.shape[0]
    acc = [None] * steps
    for i in range(CONV_STATE + steps):
        row = st_ref[i] if i < CONV_STATE else u_ref[i - CONV_STATE]
        for t in range(steps):
            j = i - t
            if 0 <= j < CONV_WIDTH and i <= CONV_STATE + t:
                term = row * w_ref[j:j + 1, :]
                acc[t] = term if acc[t] is None else acc[t] + term
        if i >= steps:
            ns[i - steps] = row
    for t in range(steps):
        c_ref[t] = acc[t] + b_ref[...]


def _conv_b_sample(u3, states, w, j, prev):
    steps, batch, _ = u3.shape
    bb = CONV_SAMPLE_BATCH
    n_layers = states.shape[0]
    first = prev is None
    extra_specs, extra_args, aliases = _stacked(prev, 4, 1)
    step_block = pl.BlockSpec((steps, bb, D_MODEL), lambda i: (0, i, 0))
    return pl.pallas_call(
        functools.partial(_conv_b_sample_body, layer=j, first=first),
        grid=(batch // bb,),
        in_specs=[step_block, pl.BlockSpec((None, CONV_STATE, bb, D_MODEL), lambda i: (j, 0, i, 0)),
                  _mat(j, CONV_WIDTH, D_MODEL), _vec(j)] + extra_specs,
        out_specs=[step_block,
                   _layer_block(first, n_layers, j, (CONV_STATE, bb, D_MODEL), lambda i: (0, i, 0))],
        out_shape=[jax.ShapeDtypeStruct((steps, batch, D_MODEL), f32),
                   jax.ShapeDtypeStruct((n_layers, CONV_STATE, batch, D_MODEL), f32)],
        input_output_aliases=aliases,
        compiler_params=_params("arbitrary"),
        name="conv_dw_sample",
    )(u3, states, w["conv_w_dw"], w["conv_b_dw"], *extra_args)


def _conv_tail(c, x, lg_ref, lb_ref, w2_ref, b2_ref):
    d = c - jnp.mean(c, axis=-1, keepdims=True)
    var = jnp.mean(d * d, axis=-1, keepdims=True)
    y = _silu(d * lax.rsqrt(var + EPS) * lg_ref[...] + lb_ref[...])
    return x + _dot(y.astype(bf16), w2_ref[...]) + b2_ref[...]


def _conv_c_body(c_ref, x_ref, lg_ref, lb_ref, w2_ref, b2_ref, o_ref):
    o_ref[...] = _conv_tail(c_ref[...], x_ref[...], lg_ref, lb_ref, w2_ref, b2_ref)


def _conv_c(c, x, w, j):
    rows = x.shape[0]
    tile = min(ROW_TILE, rows)
    return pl.pallas_call(
        _conv_c_body,
        grid=(rows // tile,),
        in_specs=[_rows(tile, D_MODEL), _rows(tile, D_MODEL), _vec(j), _vec(j), _mat(j, D_MODEL, D_MODEL), _vec(j)],
        out_specs=_rows(tile, D_MODEL),
        out_shape=jax.ShapeDtypeStruct((rows, D_MODEL), f32),
        compiler_params=_params("arbitrary"),
        name="conv_out",
    )(c, x, w["conv_ln_g"], w["conv_ln_b"], w["conv_w_pw2"], w["conv_b_pw2"])


def _gla_a_body(x_ref, g_ref, wq_ref, wk_ref, wv_ref, wr_ref, wz_ref, wg2_ref, bg_ref,
                q_ref, k_ref, v_ref, r_ref, la_ref):
    q, k, v, r, la = _gla_project(x_ref[...], g_ref, wq_ref, wk_ref, wv_ref, wr_ref, wz_ref, wg2_ref, bg_ref)
    q_ref[...] = q
    k_ref[...] = k
    v_ref[...] = v
    r_ref[...] = r
    la_ref[...] = la


def _gla_project(x, g_ref, wq_ref, wk_ref, wv_ref, wr_ref, wz_ref, wg2_ref, bg_ref):
    h = _rms(x, g_ref[...]).astype(bf16)
    q = _dot(h, wq_ref[...]) * (GLA_DK ** -0.5)
    k = _dot(h, wk_ref[...])
    v = _dot(h, wv_ref[...])
    r = _dot(h, wr_ref[...])
    z = _dot(h, wz_ref[...])
    pre = _dot(z.astype(bf16), wg2_ref[...]) + bg_ref[...]
    la = (jnp.minimum(pre, 0.0) - jnp.log1p(jnp.exp(-jnp.abs(pre)))) * (1.0 / GLA_TAU)
    return q, k, v, r, la


def _gla_a(x, w, layer, j):
    rows = x.shape[0]
    tile = min(ROW_TILE, rows)
    widths = (GLA_DKT, GLA_DKT, GLA_DVT, GLA_DVT, GLA_DKT)
    return pl.pallas_call(
        _gla_a_body,
        grid=(rows // tile,),
        in_specs=[_rows(tile, D_MODEL), _vec(layer),
                  _mat(j, D_MODEL, GLA_DKT, 0), _mat(j, D_MODEL, GLA_DKT, 1),
                  _mat(j, D_MODEL, GLA_DVT, 1), _mat(j, D_MODEL, GLA_DVT, 2),
                  _mat(j, D_MODEL, LANES), _mat(j, LANES, GLA_DKT), _vec(j, GLA_DKT)],
        out_specs=[_rows(tile, n) for n in widths],
        out_shape=[jax.ShapeDtypeStruct((rows, n), f32) for n in widths],
        compiler_params=_params("arbitrary"),
        name="gla_proj",
    )(x, w["norm_mix"], w["gla_w_in"], w["gla_w_in"], w["gla_w_in"], w["gla_w_in"], w["gla_w_z"],
      w["gla_w_gate2"], w["gla_b_gate"])


def _split_bf16(x):
    hi = x.astype(bf16)
    return hi, (x - hi.astype(f32)).astype(bf16)


def _decay_columns(la_hi, la_lo, ones):
    total = _dot_t_lhs(la_hi, ones) + _dot_t_lhs(la_lo, ones)
    return jnp.concatenate([jnp.exp(total)] * (GLA_DV // LANES), axis=1)


def _gla_prompt_body(x_ref, g_ref, wq_ref, wk_ref, wv_ref, wr_ref, wz_ref, wg2_ref, bg_ref, s0_ref,
                     go_ref, wo_ref, *rest, layer, first):
    out_ref, so_ref, state, scores = rest[-4:]
    c = pl.program_id(1)
    seqs, chunk, _ = x_ref.shape

    @pl.when(c == 0)
    def _():
        state[...] = s0_ref[...]

    tri = (lax.broadcasted_iota(jnp.int32, (chunk, chunk), 1)
           <= lax.broadcasted_iota(jnp.int32, (chunk, chunk), 0)).astype(bf16)
    ones = jnp.ones((chunk, LANES), bf16)
    rows = slice(0, chunk)

    def projection(s):
        x = x_ref[s]
        h = _rms(x, g_ref[...]).astype(bf16)
        got = {"x": x}
        parts = {"q": [], "k": [], "v": [], "r": []}

        def gate():
            z = _dot(h, wz_ref[...])
            pre = _dot(z.astype(bf16), wg2_ref[...]) + bg_ref[...]
            got["la"] = (jnp.minimum(pre, 0.0) - jnp.log1p(jnp.exp(-jnp.abs(pre)))) * (1.0 / GLA_TAU)

        def column_tile(name, w_ref, c0, width, scale):
            def piece():
                out = _dot(h, w_ref[:, c0:c0 + MXU_COLS])
                parts[name].append(out * scale if scale is not None else out)
                if c0 + MXU_COLS == width:
                    got[name] = jnp.concatenate(parts[name], axis=1)
            return piece

        pieces = [gate]
        for name, w_ref, width, scale in (("q", wq_ref, GLA_DKT, GLA_DK ** -0.5), ("k", wk_ref, GLA_DKT, None),
                                          ("v", wv_ref, GLA_DVT, None), ("r", wr_ref, GLA_DVT, None)):
            pieces += [column_tile(name, w_ref, c0, width, scale) for c0 in range(0, width, MXU_COLS)]
        return got, pieces

    cur, pieces = projection(0)
    for piece in pieces:
        piece()
    for s in range(seqs):
        nxt, pieces = projection(s + 1) if s + 1 < seqs else (None, [])
        la_hi, la_lo = _split_bf16(cur["la"])
        gated = _gla_chunk(rows, cur["q"], cur["k"], cur["v"], cur["r"], la_hi, la_lo, tri, ones,
                           state.at[s], scores.at[s], go_ref, pieces)
        out_ref[s] = cur["x"] + _dot(gated, wo_ref[...])
        cur = nxt

    @pl.when(c == pl.num_programs(1) - 1)
    def _():
        _own_layer(so_ref, layer, first)[...] = state[...]


def _gla_chunk(rows, q, k, v, r, la_hi, la_lo, tri, ones, state, scores, go_ref, between=()):
    chunk = tri.shape[0]
    blocks = chunk // GLA_SUB
    slots = GLA_HEADS * blocks
    emit_at = {}
    for n, piece in enumerate(between):
        emit_at.setdefault(n * slots // len(between), []).append(piece)
    la_hi, la_lo = la_hi[rows], la_lo[rows]
    b_all = _dot(tri, la_hi) + _dot(tri, la_lo)
    gated = []
    for h in range(GLA_HEADS):
        ks = slice(h * GLA_DK, (h + 1) * GLA_DK)
        vs = slice(h * GLA_DV, (h + 1) * GLA_DV)
        b = b_all[:, ks]
        qh = q[rows, ks]
        kh = k[rows, ks]
        vb = v[rows, vs].astype(bf16)
        b_last = b[chunk - 1:chunk, :]
        q_dec = (qh * jnp.exp(b)).astype(bf16)
        k_end = (kh * jnp.exp(b_last - b)).astype(bf16)
        s_old = state[h]
        o_inter = _dot(q_dec, s_old.astype(bf16))

        for i in range(chunk // GLA_SUB):
            r0 = i * GLA_SUB
            nk = r0 + GLA_SUB
            npad = LANES * (-(-nk // LANES))
            bq = b[r0:nk]
            bk = b[0:nk]
            if i == 0:
                q_exp, k_exp = bq, -bk
            else:
                ref = b[r0 - 1:r0, :]
                q_exp, k_exp = bq - ref, ref - bk
            qi = (qh[r0:nk] * jnp.exp(q_exp)).astype(bf16)
            ki = (kh[0:nk] * jnp.exp(k_exp)).astype(bf16)
            if nk < npad:
                ki = jnp.concatenate([ki, jnp.zeros((npad - nk, GLA_DK), bf16)], axis=0)
            s = _dot_t_rhs(qi, ki)
            causal = (lax.broadcasted_iota(jnp.int32, (GLA_SUB, npad), 1)
                      <= lax.broadcasted_iota(jnp.int32, (GLA_SUB, npad), 0) + r0)
            scores[h, r0:nk, 0:npad] = jnp.where(causal, s, 0.0).astype(bf16)
            if npad < chunk:
                scores[h, r0:nk, npad:chunk] = jnp.zeros((GLA_SUB, chunk - npad), bf16)
            for piece in emit_at.get(h * blocks + i, ()):
                piece()

        o = o_inter + _dot(scores[h], vb)
        gated.append(_gla_gate(o, r[rows, vs], go_ref))
        state[h] = s_old * _decay_columns(la_hi[:, ks], la_lo[:, ks], ones) + _dot_t_lhs(k_end, vb)
    return jnp.concatenate(gated, axis=1)


def _gla_prompt(x, states, w, layer, j, prev, batch, seq):
    chunk, seqs = GLA_CHUNK, GLA_SEQS
    n_layers = states.shape[0]
    first = prev is None
    extra_specs, extra_args, aliases = _stacked(prev, 12, 1)
    seq_tile = pl.BlockSpec((seqs, chunk, D_MODEL), lambda i, c: (i, c, 0))
    state_shape = (seqs, GLA_HEADS, GLA_DK, GLA_DV)
    out, new_states = pl.pallas_call(
        functools.partial(_gla_prompt_body, layer=j, first=first),
        grid=(batch // seqs, seq // chunk),
        in_specs=[seq_tile, _vec(layer),
                  _mat(j, D_MODEL, GLA_DKT, 0), _mat(j, D_MODEL, GLA_DKT, 1),
                  _mat(j, D_MODEL, GLA_DVT, 1), _mat(j, D_MODEL, GLA_DVT, 2),
                  _mat(j, D_MODEL, LANES), _mat(j, LANES, GLA_DKT), _vec(j, GLA_DKT),
                  pl.BlockSpec((None,) + state_shape, lambda i, c: (j, i, 0, 0, 0)),
                  _vec(j, GLA_DV), _mat(j, GLA_DVT, D_MODEL)] + extra_specs,
        out_specs=[seq_tile, _layer_block(first, n_layers, j, state_shape, lambda i, c: (i, 0, 0, 0))],
        out_shape=[jax.ShapeDtypeStruct((batch, seq, D_MODEL), f32),
                   jax.ShapeDtypeStruct((n_layers, batch, GLA_HEADS, GLA_DK, GLA_DV), f32)],
        scratch_shapes=[pltpu.VMEM(state_shape, f32), pltpu.VMEM((seqs, GLA_HEADS, chunk, chunk), bf16)],
        input_output_aliases=aliases,
        compiler_params=_params("arbitrary", "arbitrary"),
        name="gla_prompt",
    )(x.reshape(batch, seq, D_MODEL), w["norm_mix"], w["gla_w_in"], w["gla_w_in"], w["gla_w_in"], w["gla_w_in"],
      w["gla_w_z"], w["gla_w_gate2"], w["gla_b_gate"], states, w["gla_onorm_g"], w["gla_w_out"], *extra_args)
    return out.reshape(batch * seq, D_MODEL), new_states


def _gla_b_sample_body(q_ref, k_ref, v_ref, la_ref, s0_ref, *rest, layer, first):
    o_ref, so_ref = rest[-2:]
    so = _own_layer(so_ref, layer, first)
    steps, bb, _ = q_ref.shape
    causal = (lax.broadcasted_iota(jnp.int32, (steps, steps), 1)
              <= lax.broadcasted_iota(jnp.int32, (steps, steps), 0))
    ones = jnp.ones((steps, LANES), bf16)

    for i in range(bb):
        la = la_ref[:, i, :]
        q, k, v = q_ref[:, i, :], k_ref[:, i, :], v_ref[:, i, :]
        cum = [la[0:1]]
        for t in range(1, steps):
            cum.append(cum[-1] + la[t:t + 1])
        b_all = jnp.concatenate(cum, axis=0)
        la_hi, la_lo = _split_bf16(la)
        heads = []
        for h in range(GLA_HEADS):
            ks = slice(h * GLA_DK, (h + 1) * GLA_DK)
            vs = slice(h * GLA_DV, (h + 1) * GLA_DV)
            b = b_all[:, ks]
            qh = q[:, ks]
            kh = k[:, ks]
            vb = v[:, vs].astype(bf16)
            b_last = b[steps - 1:steps, :]
            q_dec = (qh * jnp.exp(b)).astype(bf16)
            k_inv = (kh * jnp.exp(-b)).astype(bf16)
            k_end = (kh * jnp.exp(b_last - b)).astype(bf16)
            s_old = s0_ref[i, h]
            sc = jnp.where(causal, _dot_t_rhs(q_dec, k_inv), 0.0).astype(bf16)
            heads.append(_dot(sc, vb) + _dot(q_dec, s_old.astype(bf16)))
            so[i, h] = s_old * _decay_columns(la_hi[:, ks], la_lo[:, ks], ones) + _dot_t_lhs(k_end, vb)
        o_ref[:, i, :] = jnp.concatenate(heads, axis=1)


def _gla_b_sample(q, k, v, la, states, j, prev):
    steps, batch, _ = q.shape
    bb = GLA_SAMPLE_BATCH
    n_layers = states.shape[0]
    first = prev is None
    extra_specs, extra_args, aliases = _stacked(prev, 5, 1)
    step_block = lambda n: pl.BlockSpec((steps, bb, n), lambda i: (0, i, 0))
    state_shape = (bb, GLA_HEADS, GLA_DK, GLA_DV)
    return pl.pallas_call(
        functools.partial(_gla_b_sample_body, layer=j, first=first),
        grid=(batch // bb,),
        in_specs=[step_block(GLA_DKT), step_block(GLA_DKT), step_block(GLA_DVT), step_block(GLA_DKT),
                  pl.BlockSpec((None,) + state_shape, lambda i: (j, i, 0, 0, 0))] + extra_specs,
        out_specs=[step_block(GLA_DVT),
                   _layer_block(first, n_layers, j, state_shape, lambda i: (i, 0, 0, 0))],
        out_shape=[jax.ShapeDtypeStruct((steps, batch, GLA_DVT), f32),
                   jax.ShapeDtypeStruct((n_layers, batch, GLA_HEADS, GLA_DK, GLA_DV), f32)],
        input_output_aliases=aliases,
        compiler_params=_params("arbitrary"),
        name="gla_core_sample",
    )(q, k, v, la, states, *extra_args)


def _gla_gate(o, r, go_ref):
    return (_rms(o, go_ref[...]) * _silu(r)).astype(bf16)


def _gla_c_body(o_ref, r_ref, x_ref, go_ref, wo_ref, out_ref):
    parts = []
    for h in range(GLA_HEADS):
        vs = slice(h * GLA_DV, (h + 1) * GLA_DV)
        parts.append(_gla_gate(o_ref[:, vs], r_ref[:, vs], go_ref))
    out_ref[...] = x_ref[...] + _dot(jnp.concatenate(parts, axis=1), wo_ref[...])


def _gla_c(o, r, x, w, j):
    rows = x.shape[0]
    tile = min(ROW_TILE, rows)
    return pl.pallas_call(
        _gla_c_body,
        grid=(rows // tile,),
        in_specs=[_rows(tile, GLA_DVT), _rows(tile, GLA_DVT), _rows(tile, D_MODEL), _vec(j, GLA_DV),
                  _mat(j, GLA_DVT, D_MODEL)],
        out_specs=_rows(tile, D_MODEL),
        out_shape=jax.ShapeDtypeStruct((rows, D_MODEL), f32),
        compiler_params=_params("arbitrary"),
        name="gla_out",
    )(o, r, x, w["gla_onorm_g"], w["gla_w_out"])


def _mix_sample(x, conv_state, gla_state, w, i, new_conv, new_gla, batch, steps):
    j = i // 2
    if i % 2 == 0:
        u = _conv_a(x, w, i, j)
        c, new_conv = _conv_b_sample(u.reshape(steps, batch, D_MODEL), conv_state, w, j, new_conv)
        x = _conv_c(c.reshape(steps * batch, D_MODEL), x, w, j)
    else:
        q, k, v, r, la = _gla_a(x, w, i, j)
        by_step = lambda a: a.reshape(steps, batch, a.shape[-1])
        o, new_gla = _gla_b_sample(by_step(q), by_step(k), by_step(v), by_step(la), gla_state, j, new_gla)
        x = _gla_c(o.reshape(steps * batch, GLA_DVT), r, x, w, j)
    return x, new_conv, new_gla


def _mix_prompt(x, conv_state, gla_state, w, i, new_conv, new_gla, batch, seq):
    j = i // 2
    if i % 2 == 0:
        x, new_conv = _conv_prompt(x, conv_state, w, i, j, new_conv, batch, seq)
    else:
        x, new_gla = _gla_prompt(x, gla_state, w, i, j, new_gla, batch, seq)
    return x, new_conv, new_gla


def kernel(x_prompt, x_sample, state_conv, state_gla, norm_ffn_pre, norm_mix, norm_ffn_post, norm_final,
           ffn_w_gate, ffn_w_up, ffn_w_down, conv_w_pw1, conv_b_pw1, conv_w_dw, conv_b_dw, conv_ln_g, conv_ln_b,
           conv_w_pw2, conv_b_pw2, gla_w_in, gla_w_gate2, gla_b_gate, gla_onorm_g, gla_w_out):
    row = lambda a: a[..., None, :]
    z_start = 2 * GLA_DKT + 2 * GLA_DVT
    w_z = jnp.pad(gla_w_in[:, :, z_start:], ((0, 0), (0, 0), (0, LANES - GLA_GATE_RANK)))
    w_gate2 = jnp.pad(gla_w_gate2, ((0, 0), (0, LANES - GLA_GATE_RANK), (0, 0)))
    w = dict(
        norm_ffn_pre=row(norm_ffn_pre), norm_mix=row(norm_mix), norm_ffn_post=row(norm_ffn_post),
        norm_final=norm_final[None, :],
        ffn_w_gate=ffn_w_gate, ffn_w_up=ffn_w_up, ffn_w_down=ffn_w_down,
        conv_w_pw1=conv_w_pw1.astype(bf16), conv_b_pw1=row(conv_b_pw1),
        conv_w_dw=conv_w_dw, conv_b_dw=row(conv_b_dw), conv_ln_g=row(conv_ln_g), conv_ln_b=row(conv_ln_b),
        conv_w_pw2=conv_w_pw2.astype(bf16), conv_b_pw2=row(conv_b_pw2),
        gla_w_in=gla_w_in.astype(bf16), gla_w_z=w_z.astype(bf16), gla_w_gate2=w_gate2.astype(bf16),
        gla_b_gate=row(gla_b_gate), gla_onorm_g=row(gla_onorm_g), gla_w_out=gla_w_out.astype(bf16),
    )
    batch, seq, _ = x_prompt.shape
    dec_batch, steps, _ = x_sample.shape
    conv0 = jnp.zeros((state_conv.shape[0], batch) + state_conv.shape[2:], x_prompt.dtype)
    gla0 = jnp.zeros((state_gla.shape[0], batch) + state_gla.shape[2:], x_prompt.dtype)
    conv_in_s = state_conv.transpose(0, 2, 1, 3)
    xp = x_prompt.reshape(batch * seq, D_MODEL)
    xs = x_sample.transpose(1, 0, 2).reshape(steps * dec_batch, D_MODEL)

    wb = tuple(a[0, 0].astype(bf16) for a in (ffn_w_gate, ffn_w_up, ffn_w_down))
    conv_p = gla_p = conv_s = gla_s = None
    for i in range(DEPTH):
        xp, xs, wb = _ffn(xp, xs, wb, w, "norm_ffn_pre", i, (i, 1))
        xp, conv_p, gla_p = _mix_prompt(xp, conv0, gla0, w, i, conv_p, gla_p, batch, seq)
        xs, conv_s, gla_s = _mix_sample(xs, conv_in_s, state_gla, w, i, conv_s, gla_s, dec_batch, steps)
        last = i == DEPTH - 1
        xp, xs, wb = _ffn(xp, xs, wb, w, "norm_ffn_post", i, None if last else (i + 1, 0), final_norm=last)

    y_p = xp.reshape(batch, seq, D_MODEL)
    y_s = xs.reshape(steps, dec_batch, D_MODEL).transpose(1, 0, 2)
    return (y_p, y_s, conv_p, gla_p, conv_s.transpose(0, 2, 1, 3), gla_s)
```

```python
import functools

import jax
import jax.numpy as jnp
from jax import lax
from jax.experimental import pallas as pl
from jax.experimental.pallas import tpu as pltpu

f32 = jnp.float32
bf16 = jnp.bfloat16

D_MODEL = 1024
D_FF = 2816
DEPTH = 4
CONV_WIDTH = 31
CONV_STATE = CONV_WIDTH - 1
GLA_HEADS = 4
GLA_DK = 128
GLA_DV = 256
GLA_DKT = GLA_HEADS * GLA_DK
GLA_DVT = GLA_HEADS * GLA_DV
GLA_GATE_RANK = 16
GLA_TAU = 16.0
EPS = 1e-6

LANES = 128
SUBLANES = 8
N_SLABS = D_MODEL // LANES
MXU_COLS = 256
VMEM_LIMIT = 56 * 1024 * 1024

ROW_TILE = 512
FFN_TILE = 1024
FFN_CHUNK = 256
CONV_TILE = 1024
CONV_HALO = 32
CONV_STRIDE = 4
CONV_BLOCK = SUBLANES * CONV_STRIDE
CONV_SAMPLE_BATCH = 8
GLA_CHUNK = 256
GLA_SEQS = 2
GLA_SUB = 32
GLA_SAMPLE_BATCH = 8


def _params(*semantics):
    return pltpu.CompilerParams(dimension_semantics=semantics, vmem_limit_bytes=VMEM_LIMIT)


def _resident(block, index=None):
    index = (0,) * len(block) if index is None else index
    return pl.BlockSpec(block, lambda *_: index, pipeline_mode=pl.Buffered(1))


def _vec(layer, width=D_MODEL, part=0):
    return _resident((None, 1, width), (layer, 0, part))


def _mat(layer, rows, cols, part=0):
    return _resident((None, rows, cols), (layer, 0, part))


def _rows(tile, width):
    return pl.BlockSpec((tile, width), lambda i: (i, 0))


def _stacked(prev, in_count, out_index):
    if prev is None:
        return [], [], {}
    return [pl.BlockSpec(memory_space=pl.ANY)], [prev], {in_count: out_index}


def _layer_block(first, n_layers, layer, block, index):
    lead, at = (n_layers, 0) if first else (None, layer)
    return pl.BlockSpec((lead,) + block, lambda *grid: (at,) + index(*grid))


def _own_layer(ref, layer, first):
    if not first:
        return ref
    for other in range(ref.shape[0]):
        if other != layer:
            ref[other] = jnp.zeros(ref.shape[1:], ref.dtype)
    return ref.at[layer]


def _rms(x, g):
    return x * lax.rsqrt(jnp.mean(x * x, axis=-1, keepdims=True) + EPS) * g


def _sigmoid(x):
    return 0.5 * jnp.tanh(0.5 * x) + 0.5


def _silu(x):
    h = 0.5 * x
    return h + h * jnp.tanh(h)


def _dot(a, b):
    return jnp.dot(a, b, preferred_element_type=f32)


def _dot_t_rhs(a, b):
    return lax.dot_general(a, b, (((1,), (1,)), ((), ())), preferred_element_type=f32)


def _dot_t_lhs(a, b):
    return lax.dot_general(a, b, (((0,), (0,)), ((), ())), preferred_element_type=f32)


def _ffn_rows(x, g_ref, wg_ref, wu_ref, wd_ref, gf_ref, a_scr, final_norm):
    rows = x.shape[0]
    h = _rms(x, g_ref[...]).astype(bf16)
    for j in range(D_FF // FFN_CHUNK):
        cols = pl.ds(j * FFN_CHUNK, FFN_CHUNK)
        gate = _dot(h, wg_ref[:, cols])
        up = _dot(h, wu_ref[:, cols])
        a_scr[0:rows, cols] = (_silu(gate) * up).astype(bf16)
    y = x + 0.5 * _dot(a_scr[0:rows, :], wd_ref[...])
    return _rms(y, gf_ref[...]) if final_norm else y


def _ffn_tile(x_ref, o_ref, g_ref, wg_ref, wu_ref, wd_ref, gf_ref, a_scr, final_norm):
    half = x_ref.shape[0] // 2
    lo, hi = slice(0, half), slice(half, 2 * half)
    chunks = [pl.ds(j * FFN_CHUNK, FFN_CHUNK) for j in range(D_FF // FFN_CHUNK)]

    def hidden(h, rows, cols):
        a_scr[rows, cols] = (_silu(_dot(h, wg_ref[:, cols])) * _dot(h, wu_ref[:, cols])).astype(bf16)

    def finish(x, rows):
        y = x + 0.5 * _dot(a_scr[rows, :], wd_ref[...])
        o_ref[rows, :] = _rms(y, gf_ref[...]) if final_norm else y

    x_lo = x_ref[lo, :]
    h_lo = _rms(x_lo, g_ref[...]).astype(bf16)
    hidden(h_lo, lo, chunks[0])
    x_hi = x_ref[hi, :]
    h_hi = _rms(x_hi, g_ref[...]).astype(bf16)
    for cols in chunks[1:]:
        hidden(h_lo, lo, cols)
    hidden(h_hi, hi, chunks[0])
    finish(x_lo, lo)
    for cols in chunks[1:]:
        hidden(h_hi, hi, cols)
    finish(x_hi, hi)


def _ffn_body(xp_ref, xs_ref, g_ref, wg_ref, wu_ref, wd_ref, gf_ref, *rest, final_norm, cast_next):
    op_ref, os_ref = rest[3:5] if cast_next else rest[0:2]
    a_scr = rest[-1]
    weights = (g_ref, wg_ref, wu_ref, wd_ref, gf_ref, a_scr, final_norm)
    _ffn_tile(xp_ref, op_ref, *weights)

    @pl.when(pl.program_id(0) == pl.num_programs(0) - 1)
    def _():
        os_ref[...] = _ffn_rows(xs_ref[...], *weights)

    if cast_next:
        for src, dst in zip(rest[0:3], rest[5:8]):
            dst[...] = src[...].astype(bf16)


def _ffn(xp, xs, wb, w, gain, layer, nxt, *, final_norm=False):
    rows = xp.shape[0]
    tile = FFN_TILE
    steps = rows // tile
    cast_next = nxt is not None
    in_specs = [_rows(tile, D_MODEL), _resident(xs.shape), _vec(layer), _resident((D_MODEL, D_FF)),
                _resident((D_MODEL, D_FF)), _resident((D_FF, D_MODEL)), _resident((1, D_MODEL))]
    out_specs = [_rows(tile, D_MODEL), pl.BlockSpec(xs.shape, lambda i: (0, 0))]
    out_shape = [jax.ShapeDtypeStruct(xp.shape, f32), jax.ShapeDtypeStruct(xs.shape, f32)]
    args = [xp, xs, w[gain], *wb, w["norm_final"]]
    if cast_next:
        for name, (r, c) in (("ffn_w_gate", (D_MODEL, D_FF)), ("ffn_w_up", (D_MODEL, D_FF)),
                             ("ffn_w_down", (D_FF, D_MODEL))):
            in_specs.append(pl.BlockSpec((None, None, r // steps, c), lambda i: nxt + (i, 0)))
            out_specs.append(pl.BlockSpec((r // steps, c), lambda i: (i, 0)))
            out_shape.append(jax.ShapeDtypeStruct((r, c), bf16))
            args.append(w[name])
    outs = pl.pallas_call(
        functools.partial(_ffn_body, final_norm=final_norm, cast_next=cast_next),
        grid=(steps,),
        in_specs=in_specs,
        out_specs=out_specs,
        out_shape=out_shape,
        scratch_shapes=[pltpu.VMEM((tile, D_FF), bf16)],
        compiler_params=_params("arbitrary"),
        name="ffn",
    )(*args)
    return outs[0], outs[1], (tuple(outs[2:]) if cast_next else None)


def _conv_glu(x, g_ref, wa_ref, wb_ref, ba_ref, bb_ref):
    h = _rms(x, g_ref[...]).astype(bf16)
    a = _dot(h, wa_ref[...]) + ba_ref[...]
    gate = _dot(h, wb_ref[...]) + bb_ref[...]
    return a * _sigmoid(gate)


def _conv_a_body(x_ref, g_ref, wa_ref, wb_ref, ba_ref, bb_ref, u_ref):
    u_ref[...] = _conv_glu(x_ref[...], g_ref, wa_ref, wb_ref, ba_ref, bb_ref)


def _conv_a(x, w, layer, j):
    rows = x.shape[0]
    tile = min(ROW_TILE, rows)
    return pl.pallas_call(
        _conv_a_body,
        grid=(rows // tile,),
        in_specs=[_rows(tile, D_MODEL), _vec(layer), _mat(j, D_MODEL, D_MODEL, 0), _mat(j, D_MODEL, D_MODEL, 1),
                  _vec(j, part=0), _vec(j, part=1)],
        out_specs=_rows(tile, D_MODEL),
        out_shape=jax.ShapeDtypeStruct((rows, D_MODEL), f32),
        compiler_params=_params("arbitrary"),
        name="conv_glu",
    )(x, w["norm_mix"], w["conv_w_pw1"], w["conv_w_pw1"], w["conv_b_pw1"], w["conv_b_pw1"])


def _conv_prompt_body(x_ref, g_ref, wa_ref, wb_ref, ba_ref, bb_ref, st_ref, w_ref, b_ref,
                      lg_ref, lb_ref, w2_ref, b2_ref, *rest, layer, first):
    o_ref, ns_ref, hist, cout, taps = rest[-5:]
    t = pl.program_id(1)
    tile = x_ref.shape[0]

    @pl.when(t == 0)
    def _():
        for l in range(N_SLABS):
            lanes = slice(l * LANES, (l + 1) * LANES)
            hist[l, CONV_HALO - CONV_STATE:CONV_HALO, :] = st_ref[0, :, lanes]
            for j in range(CONV_WIDTH):
                taps[l, j] = jnp.broadcast_to(w_ref[j:j + 1, lanes], (2 * SUBLANES, LANES)).astype(bf16)

    @pl.when(t > 0)
    def _():
        for l in range(N_SLABS):
            hist[l, 0:CONV_HALO, :] = hist[l, tile:tile + CONV_HALO, :]

    x = x_ref[...]
    u = _conv_glu(x, g_ref, wa_ref, wb_ref, ba_ref, bb_ref)
    first_tap = CONV_HALO - CONV_STATE
    for l in range(N_SLABS):
        lanes = slice(l * LANES, (l + 1) * LANES)
        hist[l, CONV_HALO:CONV_HALO + tile, :] = u[:, lanes]

        def block_pair(pair, carry, l=l):
            bases = [(pair * 2 + half) * CONV_BLOCK for half in range(2)]
            acc = [None] * CONV_STRIDE
            for shift in range(first_tap, first_tap + CONV_WIDTH + CONV_STRIDE - 1):
                rows = jnp.concatenate(
                    [hist[l, pl.ds(base + shift, SUBLANES, stride=CONV_STRIDE), :] for base in bases],
                    axis=0).astype(bf16)
                for t0 in range(CONV_STRIDE):
                    j = shift - t0 - first_tap
                    if 0 <= j < CONV_WIDTH:
                        term = rows.astype(f32) * taps[l, j].astype(f32)
                        acc[t0] = term if acc[t0] is None else acc[t0] + term
            for t0 in range(CONV_STRIDE):
                for half, base in enumerate(bases):
                    cout[l, pl.ds(base + t0, SUBLANES, stride=CONV_STRIDE), :] = (
                        acc[t0][half * SUBLANES:(half + 1) * SUBLANES])
            return carry

        lax.fori_loop(0, tile // (2 * CONV_BLOCK), block_pair, 0)

    c = jnp.concatenate([cout[l] for l in range(N_SLABS)], axis=1) + b_ref[...]
    o_ref[...] = _conv_tail(c, x, lg_ref, lb_ref, w2_ref, b2_ref)
    _own_layer(ns_ref, layer, first)[0] = jnp.concatenate(
        [hist[l, tile + CONV_HALO - CONV_STATE:tile + CONV_HALO, :] for l in range(N_SLABS)], axis=1)


def _conv_prompt(x, states, w, layer, j, prev, batch, seq):
    tile = CONV_TILE
    nt = seq // tile
    n_layers = states.shape[0]
    extra_specs, extra_args, aliases = _stacked(prev, 13, 1)
    seq_tile = pl.BlockSpec((tile, D_MODEL), lambda i, t: (i * nt + t, 0))
    state_block = pl.BlockSpec((None, 1, CONV_STATE, D_MODEL), lambda i, t: (j, i, 0, 0))
    return pl.pallas_call(
        functools.partial(_conv_prompt_body, layer=j, first=prev is None),
        grid=(batch, nt),
        in_specs=[seq_tile, _vec(layer), _mat(j, D_MODEL, D_MODEL, 0), _mat(j, D_MODEL, D_MODEL, 1),
                  _vec(j, part=0), _vec(j, part=1), state_block, _mat(j, CONV_WIDTH, D_MODEL), _vec(j),
                  _vec(j), _vec(j), _mat(j, D_MODEL, D_MODEL), _vec(j)] + extra_specs,
        out_specs=[seq_tile, _layer_block(prev is None, n_layers, j, (1, CONV_STATE, D_MODEL),
                                          lambda i, t: (i, 0, 0))],
        out_shape=[jax.ShapeDtypeStruct((batch * seq, D_MODEL), f32),
                   jax.ShapeDtypeStruct((n_layers, batch, CONV_STATE, D_MODEL), f32)],
        scratch_shapes=[pltpu.VMEM((N_SLABS, CONV_HALO + tile, LANES), f32),
                        pltpu.VMEM((N_SLABS, tile, LANES), f32),
                        pltpu.VMEM((N_SLABS, CONV_WIDTH, 2 * SUBLANES, LANES), bf16)],
        input_output_aliases=aliases,
        compiler_params=_params("arbitrary", "arbitrary"),
        name="conv_prompt",
    )(x, w["norm_mix"], w["conv_w_pw1"], w["conv_w_pw1"], w["conv_b_pw1"], w["conv_b_pw1"], states,
      w["conv_w_dw"], w["conv_b_dw"], w["conv_ln_g"], w["conv_ln_b"], w["conv_w_pw2"], w["conv_b_pw2"],
      *extra_args)


def _conv_b_sample_body(u_ref, st_ref, w_ref, b_ref, *rest, layer, first):
    c_ref, ns_ref = rest[-2:]
    ns = _own_layer(ns_ref, layer, first)
    steps = u_ref.shape[0]
    acc = [None] * steps
    for i in range(CONV_STATE + steps):
        row = st_ref[i] if i < CONV_STATE else u_ref[i - CONV_STATE]
        for t in range(steps):
            j = i - t
            if 0 <= j < CONV_WIDTH and i <= CONV_STATE + t:
                term = row * w_ref[j:j + 1, :]
                acc[t] = term if acc[t] is None else acc[t] + term
        if i >= steps:
            ns[i - steps] = row
    for t in range(steps):
        c_ref[t] = acc[t] + b_ref[...]


def _conv_b_sample(u3, states, w, j, prev):
    steps, batch, _ = u3.shape
    bb = CONV_SAMPLE_BATCH
    n_layers = states.shape[0]
    first = prev is None
    extra_specs, extra_args, aliases = _stacked(prev, 4, 1)
    step_block = pl.BlockSpec((steps, bb, D_MODEL), lambda i: (0, i, 0))
    return pl.pallas_call(
        functools.partial(_conv_b_sample_body, layer=j, first=first),
        grid=(batch // bb,),
        in_specs=[step_block, pl.BlockSpec((None, CONV_STATE, bb, D_MODEL), lambda i: (j, 0, i, 0)),
                  _mat(j, CONV_WIDTH, D_MODEL), _vec(j)] + extra_specs,
        out_specs=[step_block,
                   _layer_block(first, n_layers, j, (CONV_STATE, bb, D_MODEL), lambda i: (0, i, 0))],
        out_shape=[jax.ShapeDtypeStruct((steps, batch, D_MODEL), f32),
                   jax.ShapeDtypeStruct((n_layers, CONV_STATE, batch, D_MODEL), f32)],
        input_output_aliases=aliases,
        compiler_params=_params("arbitrary"),
        name="conv_dw_sample",
    )(u3, states, w["conv_w_dw"], w["conv_b_dw"], *extra_args)


def _conv_tail(c, x, lg_ref, lb_ref, w2_ref, b2_ref):
    d = c - jnp.mean(c, axis=-1, keepdims=True)
    var = jnp.mean(d * d, axis=-1, keepdims=True)
    y = _silu(d * lax.rsqrt(var + EPS) * lg_ref[...] + lb_ref[...])
    return x + _dot(y.astype(bf16), w2_ref[...]) + b2_ref[...]


def _conv_c_body(c_ref, x_ref, lg_ref, lb_ref, w2_ref, b2_ref, o_ref):
    o_ref[...] = _conv_tail(c_ref[...], x_ref[...], lg_ref, lb_ref, w2_ref, b2_ref)


def _conv_c(c, x, w, j):
    rows = x.shape[0]
    tile = min(ROW_TILE, rows)
    return pl.pallas_call(
        _conv_c_body,
        grid=(rows // tile,),
        in_specs=[_rows(tile, D_MODEL), _rows(tile, D_MODEL), _vec(j), _vec(j), _mat(j, D_MODEL, D_MODEL), _vec(j)],
        out_specs=_rows(tile, D_MODEL),
        out_shape=jax.ShapeDtypeStruct((rows, D_MODEL), f32),
        compiler_params=_params("arbitrary"),
        name="conv_out",
    )(c, x, w["conv_ln_g"], w["conv_ln_b"], w["conv_w_pw2"], w["conv_b_pw2"])


def _gla_a_body(x_ref, g_ref, wq_ref, wk_ref, wv_ref, wr_ref, wz_ref, wg2_ref, bg_ref,
                q_ref, k_ref, v_ref, r_ref, la_ref):
    q, k, v, r, la = _gla_project(x_ref[...], g_ref, wq_ref, wk_ref, wv_ref, wr_ref, wz_ref, wg2_ref, bg_ref)
    q_ref[...] = q
    k_ref[...] = k
    v_ref[...] = v
    r_ref[...] = r
    la_ref[...] = la


def _gla_project(x, g_ref, wq_ref, wk_ref, wv_ref, wr_ref, wz_ref, wg2_ref, bg_ref):
    h = _rms(x, g_ref[...]).astype(bf16)
    q = _dot(h, wq_ref[...]) * (GLA_DK ** -0.5)
    k = _dot(h, wk_ref[...])
    v = _dot(h, wv_ref[...])
    r = _dot(h, wr_ref[...])
    z = _dot(h, wz_ref[...])
    pre = _dot(z.astype(bf16), wg2_ref[...]) + bg_ref[...]
    la = (jnp.minimum(pre, 0.0) - jnp.log1p(jnp.exp(-jnp.abs(pre)))) * (1.0 / GLA_TAU)
    return q, k, v, r, la


def _gla_a(x, w, layer, j):
    rows = x.shape[0]
    tile = min(ROW_TILE, rows)
    widths = (GLA_DKT, GLA_DKT, GLA_DVT, GLA_DVT, GLA_DKT)
    return pl.pallas_call(
        _gla_a_body,
        grid=(rows // tile,),
        in_specs=[_rows(tile, D_MODEL), _vec(layer),
                  _mat(j, D_MODEL, GLA_DKT, 0), _mat(j, D_MODEL, GLA_DKT, 1),
                  _mat(j, D_MODEL, GLA_DVT, 1), _mat(j, D_MODEL, GLA_DVT, 2),
                  _mat(j, D_MODEL, LANES), _mat(j, LANES, GLA_DKT), _vec(j, GLA_DKT)],
        out_specs=[_rows(tile, n) for n in widths],
        out_shape=[jax.ShapeDtypeStruct((rows, n), f32) for n in widths],
        compiler_params=_params("arbitrary"),
        name="gla_proj",
    )(x, w["norm_mix"], w["gla_w_in"], w["gla_w_in"], w["gla_w_in"], w["gla_w_in"], w["gla_w_z"],
      w["gla_w_gate2"], w["gla_b_gate"])


def _split_bf16(x):
    hi = x.astype(bf16)
    return hi, (x - hi.astype(f32)).astype(bf16)


def _decay_columns(la_hi, la_lo, ones):
    total = _dot_t_lhs(la_hi, ones) + _dot_t_lhs(la_lo, ones)
    return jnp.concatenate([jnp.exp(total)] * (GLA_DV // LANES), axis=1)


def _gla_prompt_body(x_ref, g_ref, wq_ref, wk_ref, wv_ref, wr_ref, wz_ref, wg2_ref, bg_ref, s0_ref,
                     go_ref, wo_ref, *rest, layer, first):
    out_ref, so_ref, state, scores = rest[-4:]
    c = pl.program_id(1)
    seqs, chunk, _ = x_ref.shape

    @pl.when(c == 0)
    def _():
        state[...] = s0_ref[...]

    tri = (lax.broadcasted_iota(jnp.int32, (chunk, chunk), 1)
           <= lax.broadcasted_iota(jnp.int32, (chunk, chunk), 0)).astype(bf16)
    ones = jnp.ones((chunk, LANES), bf16)
    rows = slice(0, chunk)

    def projection(s):
        x = x_ref[s]
        h = _rms(x, g_ref[...]).astype(bf16)
        got = {"x": x}
        parts = {"q": [], "k": [], "v": [], "r": []}

        def gate():
            z = _dot(h, wz_ref[...])
            pre = _dot(z.astype(bf16), wg2_ref[...]) + bg_ref[...]
            got["la"] = (jnp.minimum(pre, 0.0) - jnp.log1p(jnp.exp(-jnp.abs(pre)))) * (1.0 / GLA_TAU)

        def column_tile(name, w_ref, c0, width, scale):
            def piece():
                out = _dot(h, w_ref[:, c0:c0 + MXU_COLS])
                parts[name].append(out * scale if scale is not None else out)
                if c0 + MXU_COLS == width:
                    got[name] = jnp.concatenate(parts[name], axis=1)
            return piece

        pieces = [gate]
        for name, w_ref, width, scale in (("q", wq_ref, GLA_DKT, GLA_DK ** -0.5), ("k", wk_ref, GLA_DKT, None),
                                          ("v", wv_ref, GLA_DVT, None), ("r", wr_ref, GLA_DVT, None)):
            pieces += [column_tile(name, w_ref, c0, width, scale) for c0 in range(0, width, MXU_COLS)]
        return got, pieces

    cur, pieces = projection(0)
    for piece in pieces:
        piece()
    for s in range(seqs):
        nxt, pieces = projection(s + 1) if s + 1 < seqs else (None, [])
        la_hi, la_lo = _split_bf16(cur["la"])
        gated = _gla_chunk(rows, cur["q"], cur["k"], cur["v"], cur["r"], la_hi, la_lo, tri, ones,
                           state.at[s], scores.at[s], go_ref, pieces)
        out_ref[s] = cur["x"] + _dot(gated, wo_ref[...])
        cur = nxt

    @pl.when(c == pl.num_programs(1) - 1)
    def _():
        _own_layer(so_ref, layer, first)[...] = state[...]


def _gla_chunk(rows, q, k, v, r, la_hi, la_lo, tri, ones, state, scores, go_ref, between=()):
    chunk = tri.shape[0]
    blocks = chunk // GLA_SUB
    slots = GLA_HEADS * blocks
    emit_at = {}
    for n, piece in enumerate(between):
        emit_at.setdefault(n * slots // len(between), []).append(piece)
    la_hi, la_lo = la_hi[rows], la_lo[rows]
    b_all = _dot(tri, la_hi) + _dot(tri, la_lo)
    gated = []
    for h in range(GLA_HEADS):
        ks = slice(h * GLA_DK, (h + 1) * GLA_DK)
        vs = slice(h * GLA_DV, (h + 1) * GLA_DV)
        b = b_all[:, ks]
        qh = q[rows, ks]
        kh = k[rows, ks]
        vb = v[rows, vs].astype(bf16)
        b_last = b[chunk - 1:chunk, :]
        q_dec = (qh * jnp.exp(b)).astype(bf16)
        k_end = (kh * jnp.exp(b_last - b)).astype(bf16)
        s_old = state[h]
        o_inter = _dot(q_dec, s_old.astype(bf16))

        for i in range(chunk // GLA_SUB):
            r0 = i * GLA_SUB
            nk = r0 + GLA_SUB
            npad = LANES * (-(-nk // LANES))
            bq = b[r0:nk]
            bk = b[0:nk]
            if i == 0:
                q_exp, k_exp = bq, -bk
            else:
                ref = b[r0 - 1:r0, :]
                q_exp, k_exp = bq - ref, ref - bk
            qi = (qh[r0:nk] * jnp.exp(q_exp)).astype(bf16)
            ki = (kh[0:nk] * jnp.exp(k_exp)).astype(bf16)
            if nk < npad:
                ki = jnp.concatenate([ki, jnp.zeros((npad - nk, GLA_DK), bf16)], axis=0)
            s = _dot_t_rhs(qi, ki)
            causal = (lax.broadcasted_iota(jnp.int32, (GLA_SUB, npad), 1)
                      <= lax.broadcasted_iota(jnp.int32, (GLA_SUB, npad), 0) + r0)
            scores[h, r0:nk, 0:npad] = jnp.where(causal, s, 0.0).astype(bf16)
            if npad < chunk:
                scores[h, r0:nk, npad:chunk] = jnp.zeros((GLA_SUB, chunk - npad), bf16)
            for piece in emit_at.get(h * blocks + i, ()):
                piece()

        o = o_inter + _dot(scores[h], vb)
        gated.append(_gla_gate(o, r[rows, vs], go_ref))
        state[h] = s_old * _decay_columns(la_hi[:, ks], la_lo[:, ks], ones) + _dot_t_lhs(k_end, vb)
    return jnp.concatenate(gated, axis=1)


def _gla_prompt(x, states, w, layer, j, prev, batch, seq):
    chunk, seqs = GLA_CHUNK, GLA_SEQS
    n_layers = states.shape[0]
    first = prev is None
    extra_specs, extra_args, aliases = _stacked(prev, 12, 1)
    seq_tile = pl.BlockSpec((seqs, chunk, D_MODEL), lambda i, c: (i, c, 0))
    state_shape = (seqs, GLA_HEADS, GLA_DK, GLA_DV)
    out, new_states = pl.pallas_call(
        functools.partial(_gla_prompt_body, layer=j, first=first),
        grid=(batch // seqs, seq // chunk),
        in_specs=[seq_tile, _vec(layer),
                  _mat(j, D_MODEL, GLA_DKT, 0), _mat(j, D_MODEL, GLA_DKT, 1),
                  _mat(j, D_MODEL, GLA_DVT, 1), _mat(j, D_MODEL, GLA_DVT, 2),
                  _mat(j, D_MODEL, LANES), _mat(j, LANES, GLA_DKT), _vec(j, GLA_DKT),
                  pl.BlockSpec((None,) + state_shape, lambda i, c: (j, i, 0, 0, 0)),
                  _vec(j, GLA_DV), _mat(j, GLA_DVT, D_MODEL)] + extra_specs,
        out_specs=[seq_tile, _layer_block(first, n_layers, j, state_shape, lambda i, c: (i, 0, 0, 0))],
        out_shape=[jax.ShapeDtypeStruct((batch, seq, D_MODEL), f32),
                   jax.ShapeDtypeStruct((n_layers, batch, GLA_HEADS, GLA_DK, GLA_DV), f32)],
        scratch_shapes=[pltpu.VMEM(state_shape, f32), pltpu.VMEM((seqs, GLA_HEADS, chunk, chunk), bf16)],
        input_output_aliases=aliases,
        compiler_params=_params("arbitrary", "arbitrary"),
        name="gla_prompt",
    )(x.reshape(batch, seq, D_MODEL), w["norm_mix"], w["gla_w_in"], w["gla_w_in"], w["gla_w_in"], w["gla_w_in"],
      w["gla_w_z"], w["gla_w_gate2"], w["gla_b_gate"], states, w["gla_onorm_g"], w["gla_w_out"], *extra_args)
    return out.reshape(batch * seq, D_MODEL), new_states


def _gla_b_sample_body(q_ref, k_ref, v_ref, la_ref, s0_ref, *rest, layer, first):
    o_ref, so_ref = rest[-2:]
    so = _own_layer(so_ref, layer, first)
    steps, bb, _ = q_ref.shape
    causal = (lax.broadcasted_iota(jnp.int32, (steps, steps), 1)
              <= lax.broadcasted_iota(jnp.int32, (steps, steps), 0))
    ones = jnp.ones((steps, LANES), bf16)

    for i in range(bb):
        la = la_ref[:, i, :]
        q, k, v = q_ref[:, i, :], k_ref[:, i, :], v_ref[:, i, :]
        cum = [la[0:1]]
        for t in range(1, steps):
            cum.append(cum[-1] + la[t:t + 1])
        b_all = jnp.concatenate(cum, axis=0)
        la_hi, la_lo = _split_bf16(la)
        heads = []
        for h in range(GLA_HEADS):
            ks = slice(h * GLA_DK, (h + 1) * GLA_DK)
            vs = slice(h * GLA_DV, (h + 1) * GLA_DV)
            b = b_all[:, ks]
            qh = q[:, ks]
            kh = k[:, ks]
            vb = v[:, vs].astype(bf16)
            b_last = b[steps - 1:steps, :]
            q_dec = (qh * jnp.exp(b)).astype(bf16)
            k_inv = (kh * jnp.exp(-b)).astype(bf16)
            k_end = (kh * jnp.exp(b_last - b)).astype(bf16)
            s_old = s0_ref[i, h]
            sc = jnp.where(causal, _dot_t_rhs(q_dec, k_inv), 0.0).astype(bf16)
            heads.append(_dot(sc, vb) + _dot(q_dec, s_old.astype(bf16)))
            so[i, h] = s_old * _decay_columns(la_hi[:, ks], la_lo[:, ks], ones) + _dot_t_lhs(k_end, vb)
        o_ref[:, i, :] = jnp.concatenate(heads, axis=1)


def _gla_b_sample(q, k, v, la, states, j, prev):
    steps, batch, _ = q.shape
    bb = GLA_SAMPLE_BATCH
    n_layers = states.shape[0]
    first = prev is None
    extra_specs, extra_args, aliases = _stacked(prev, 5, 1)
    step_block = lambda n: pl.BlockSpec((steps, bb, n), lambda i: (0, i, 0))
    state_shape = (bb, GLA_HEADS, GLA_DK, GLA_DV)
    return pl.pallas_call(
        functools.partial(_gla_b_sample_body, layer=j, first=first),
        grid=(batch // bb,),
        in_specs=[step_block(GLA_DKT), step_block(GLA_DKT), step_block(GLA_DVT), step_block(GLA_DKT),
                  pl.BlockSpec((None,) + state_shape, lambda i: (j, i, 0, 0, 0))] + extra_specs,
        out_specs=[step_block(GLA_DVT),
                   _layer_block(first, n_layers, j, state_shape, lambda i: (i, 0, 0, 0))],
        out_shape=[jax.ShapeDtypeStruct((steps, batch, GLA_DVT), f32),
                   jax.ShapeDtypeStruct((n_layers, batch, GLA_HEADS, GLA_DK, GLA_DV), f32)],
        input_output_aliases=aliases,
        compiler_params=_params("arbitrary"),
        name="gla_core_sample",
    )(q, k, v, la, states, *extra_args)


def _gla_gate(o, r, go_ref):
    return (_rms(o, go_ref[...]) * _silu(r)).astype(bf16)


def _gla_c_body(o_ref, r_ref, x_ref, go_ref, wo_ref, out_ref):
    parts = []
    for h in range(GLA_HEADS):
        vs = slice(h * GLA_DV, (h + 1) * GLA_DV)
        parts.append(_gla_gate(o_ref[:, vs], r_ref[:, vs], go_ref))
    out_ref[...] = x_ref[...] + _dot(jnp.concatenate(parts, axis=1), wo_ref[...])


def _gla_c(o, r, x, w, j):
    rows = x.shape[0]
    tile = min(ROW_TILE, rows)
    return pl.pallas_call(
        _gla_c_body,
        grid=(rows // tile,),
        in_specs=[_rows(tile, GLA_DVT), _rows(tile, GLA_DVT), _rows(tile, D_MODEL), _vec(j, GLA_DV),
                  _mat(j, GLA_DVT, D_MODEL)],
        out_specs=_rows(tile, D_MODEL),
        out_shape=jax.ShapeDtypeStruct((rows, D_MODEL), f32),
        compiler_params=_params("arbitrary"),
        name="gla_out",
    )(o, r, x, w["gla_onorm_g"], w["gla_w_out"])


def _mix_sample(x, conv_state, gla_state, w, i, new_conv, new_gla, batch, steps):
    j = i // 2
    if i % 2 == 0:
        u = _conv_a(x, w, i, j)
        c, new_conv = _conv_b_sample(u.reshape(steps, batch, D_MODEL), conv_state, w, j, new_conv)
        x = _conv_c(c.reshape(steps * batch, D_MODEL), x, w, j)
    else:
        q, k, v, r, la = _gla_a(x, w, i, j)
        by_step = lambda a: a.reshape(steps, batch, a.shape[-1])
        o, new_gla = _gla_b_sample(by_step(q), by_step(k), by_step(v), by_step(la), gla_state, j, new_gla)
        x = _gla_c(o.reshape(steps * batch, GLA_DVT), r, x, w, j)
    return x, new_conv, new_gla


def _mix_prompt(x, conv_state, gla_state, w, i, new_conv, new_gla, batch, seq):
    j = i // 2
    if i % 2 == 0:
        x, new_conv = _conv_prompt(x, conv_state, w, i, j, new_conv, batch, seq)
    else:
        x, new_gla = _gla_prompt(x, gla_state, w, i, j, new_gla, batch, seq)
    return x, new_conv, new_gla


def kernel(x_prompt, x_sample, state_conv, state_gla, norm_ffn_pre, norm_mix, norm_ffn_post, norm_final,
           ffn_w_gate, ffn_w_up, ffn_w_down, conv_w_pw1, conv_b_pw1, conv_w_dw, conv_b_dw, conv_ln_g, conv_ln_b,
           conv_w_pw2, conv_b_pw2, gla_w_in, gla_w_gate2, gla_b_gate, gla_onorm_g, gla_w_out):
    row = lambda a: a[..., None, :]
    z_start = 2 * GLA_DKT + 2 * GLA_DVT
    w_z = jnp.pad(gla_w_in[:, :, z_start:], ((0, 0), (0, 0), (0, LANES - GLA_GATE_RANK)))
    w_gate2 = jnp.pad(gla_w_gate2, ((0, 0), (0, LANES - GLA_GATE_RANK), (0, 0)))
    w = dict(
        norm_ffn_pre=row(norm_ffn_pre), norm_mix=row(norm_mix), norm_ffn_post=row(norm_ffn_post),
        norm_final=norm_final[None, :],
        ffn_w_gate=ffn_w_gate, ffn_w_up=ffn_w_up, ffn_w_down=ffn_w_down,
        conv_w_pw1=conv_w_pw1.astype(bf16), conv_b_pw1=row(conv_b_pw1),
        conv_w_dw=conv_w_dw, conv_b_dw=row(conv_b_dw), conv_ln_g=row(conv_ln_g), conv_ln_b=row(conv_ln_b),
        conv_w_pw2=conv_w_pw2.astype(bf16), conv_b_pw2=row(conv_b_pw2),
        gla_w_in=gla_w_in.astype(bf16), gla_w_z=w_z.astype(bf16), gla_w_gate2=w_gate2.astype(bf16),
        gla_b_gate=row(gla_b_gate), gla_onorm_g=row(gla_onorm_g), gla_w_out=gla_w_out.astype(bf16),
    )
    batch, seq, _ = x_prompt.shape
    dec_batch, steps, _ = x_sample.shape
    conv0 = jnp.zeros((state_conv.shape[0], batch) + state_conv.shape[2:], x_prompt.dtype)
    gla0 = jnp.zeros((state_gla.shape[0], batch) + state_gla.shape[2:], x_prompt.dtype)
    conv_in_s = state_conv.transpose(0, 2, 1, 3)
    xp = x_prompt.reshape(batch * seq, D_MODEL)
    xs = x_sample.transpose(1, 0, 2).reshape(steps * dec_batch, D_MODEL)

    wb = tuple(a[0, 0].astype(bf16) for a in (ffn_w_gate, ffn_w_up, ffn_w_down))
    conv_p = gla_p = conv_s = gla_s = None
    for i in range(DEPTH):
        xp, xs, wb = _ffn(xp, xs, wb, w, "norm_ffn_pre", i, (i, 1))
        xp, conv_p, gla_p = _mix_prompt(xp, conv0, gla0, w, i, conv_p, gla_p, batch, seq)
        xs, conv_s, gla_s = _mix_sample(xs, conv_in_s, state_gla, w, i, conv_s, gla_s, dec_batch, steps)
        last = i == DEPTH - 1
        xp, xs, wb = _ffn(xp, xs, wb, w, "norm_ffn_post", i, None if last else (i + 1, 0), final_norm=last)

    y_p = xp.reshape(batch, seq, D_MODEL)
    y_s = xs.reshape(steps, dec_batch, D_MODEL).transpose(1, 0, 2)
    return (y_p, y_s, conv_p, gla_p, conv_s.transpose(0, 2, 1, 3), gla_s)
```

```python
import functools

import jax
import jax.numpy as jnp
from jax import lax
from jax.experimental import pallas as pl
from jax.experimental.pallas import tpu as pltpu

f32 = jnp.float32
bf16 = jnp.bfloat16

D_MODEL = 1024
D_FF = 2816
DEPTH = 4
CONV_WIDTH = 31
CONV_STATE = CONV_WIDTH - 1
GLA_HEADS = 4
GLA_DK = 128
GLA_DV = 256
GLA_DKT = GLA_HEADS * GLA_DK
GLA_DVT = GLA_HEADS * GLA_DV
GLA_GATE_RANK = 16
GLA_TAU = 16.0
EPS = 1e-6

LANES = 128
SUBLANES = 8
N_SLABS = D_MODEL // LANES
MXU_COLS = 256
VMEM_LIMIT = 56 * 1024 * 1024

ROW_TILE = 512
FFN_TILE = 1024
FFN_CHUNK = 256
CONV_TILE = 1024
CONV_HALO = 32
CONV_STRIDE = 4
CONV_BLOCK = SUBLANES * CONV_STRIDE
CONV_SAMPLE_BATCH = 8
GLA_CHUNK = 256
GLA_SEQS = 2
GLA_SUB = 32
GLA_SAMPLE_BATCH = 8
GLA_RING = 3


def _params(*semantics):
    return pltpu.CompilerParams(dimension_semantics=semantics, vmem_limit_bytes=VMEM_LIMIT)


def _resident(block, index=None):
    index = (0,) * len(block) if index is None else index
    return pl.BlockSpec(block, lambda *_: index, pipeline_mode=pl.Buffered(1))


def _vec(layer, width=D_MODEL, part=0):
    return _resident((None, 1, width), (layer, 0, part))


def _mat(layer, rows, cols, part=0):
    return _resident((None, rows, cols), (layer, 0, part))


def _rows(tile, width):
    return pl.BlockSpec((tile, width), lambda i: (i, 0))


def _stacked(prev, in_count, out_index):
    if prev is None:
        return [], [], {}
    return [pl.BlockSpec(memory_space=pl.ANY)], [prev], {in_count: out_index}


def _layer_block(first, n_layers, layer, block, index):
    lead, at = (n_layers, 0) if first else (None, layer)
    return pl.BlockSpec((lead,) + block, lambda *grid: (at,) + index(*grid))


def _own_layer(ref, layer, first):
    if not first:
        return ref
    for other in range(ref.shape[0]):
        if other != layer:
            ref[other] = jnp.zeros(ref.shape[1:], ref.dtype)
    return ref.at[layer]


def _rms(x, g):
    return x * lax.rsqrt(jnp.mean(x * x, axis=-1, keepdims=True) + EPS) * g


def _sigmoid(x):
    return 0.5 * jnp.tanh(0.5 * x) + 0.5


def _silu(x):
    h = 0.5 * x
    return h + h * jnp.tanh(h)


def _dot(a, b):
    return jnp.dot(a, b, preferred_element_type=f32)


def _dot_t_rhs(a, b):
    return lax.dot_general(a, b, (((1,), (1,)), ((), ())), preferred_element_type=f32)


def _dot_t_lhs(a, b):
    return lax.dot_general(a, b, (((0,), (0,)), ((), ())), preferred_element_type=f32)


def _ffn_rows(x, g_ref, wg_ref, wu_ref, wd_ref, gf_ref, a_scr, final_norm):
    rows = x.shape[0]
    h = _rms(x, g_ref[...]).astype(bf16)
    for j in range(D_FF // FFN_CHUNK):
        cols = pl.ds(j * FFN_CHUNK, FFN_CHUNK)
        gate = _dot(h, wg_ref[:, cols])
        up = _dot(h, wu_ref[:, cols])
        a_scr[0:rows, cols] = (_silu(gate) * up).astype(bf16)
    y = x + 0.5 * _dot(a_scr[0:rows, :], wd_ref[...])
    return _rms(y, gf_ref[...]) if final_norm else y


def _ffn_tile(x_ref, o_ref, g_ref, wg_ref, wu_ref, wd_ref, gf_ref, a_scr, final_norm):
    half = x_ref.shape[0] // 2
    lo, hi = slice(0, half), slice(half, 2 * half)
    chunks = [pl.ds(j * FFN_CHUNK, FFN_CHUNK) for j in range(D_FF // FFN_CHUNK)]

    def hidden(h, rows, cols):
        a_scr[rows, cols] = (_silu(_dot(h, wg_ref[:, cols])) * _dot(h, wu_ref[:, cols])).astype(bf16)

    def finish(x, rows):
        y = x + 0.5 * _dot(a_scr[rows, :], wd_ref[...])
        o_ref[rows, :] = _rms(y, gf_ref[...]) if final_norm else y

    x_lo = x_ref[lo, :]
    h_lo = _rms(x_lo, g_ref[...]).astype(bf16)
    hidden(h_lo, lo, chunks[0])
    x_hi = x_ref[hi, :]
    h_hi = _rms(x_hi, g_ref[...]).astype(bf16)
    for cols in chunks[1:]:
        hidden(h_lo, lo, cols)
    hidden(h_hi, hi, chunks[0])
    finish(x_lo, lo)
    for cols in chunks[1:]:
        hidden(h_hi, hi, cols)
    finish(x_hi, hi)


def _ffn_body(xp_ref, xs_ref, g_ref, wg_ref, wu_ref, wd_ref, gf_ref, *rest, final_norm, cast_next):
    op_ref, os_ref = rest[3:5] if cast_next else rest[0:2]
    a_scr = rest[-1]
    weights = (g_ref, wg_ref, wu_ref, wd_ref, gf_ref, a_scr, final_norm)
    _ffn_tile(xp_ref, op_ref, *weights)

    @pl.when(pl.program_id(0) == pl.num_programs(0) - 1)
    def _():
        os_ref[...] = _ffn_rows(xs_ref[...], *weights)

    if cast_next:
        for src, dst in zip(rest[0:3], rest[5:8]):
            dst[...] = src[...].astype(bf16)


def _ffn(xp, xs, wb, w, gain, layer, nxt, *, final_norm=False):
    rows = xp.shape[0]
    tile = FFN_TILE
    steps = rows // tile
    cast_next = nxt is not None
    in_specs = [_rows(tile, D_MODEL), _resident(xs.shape), _vec(layer), _resident((D_MODEL, D_FF)),
                _resident((D_MODEL, D_FF)), _resident((D_FF, D_MODEL)), _resident((1, D_MODEL))]
    out_specs = [_rows(tile, D_MODEL), pl.BlockSpec(xs.shape, lambda i: (0, 0))]
    out_shape = [jax.ShapeDtypeStruct(xp.shape, f32), jax.ShapeDtypeStruct(xs.shape, f32)]
    args = [xp, xs, w[gain], *wb, w["norm_final"]]
    if cast_next:
        for name, (r, c) in (("ffn_w_gate", (D_MODEL, D_FF)), ("ffn_w_up", (D_MODEL, D_FF)),
                             ("ffn_w_down", (D_FF, D_MODEL))):
            in_specs.append(pl.BlockSpec((None, None, r // steps, c), lambda i: nxt + (i, 0)))
            out_specs.append(pl.BlockSpec((r // steps, c), lambda i: (i, 0)))
            out_shape.append(jax.ShapeDtypeStruct((r, c), bf16))
            args.append(w[name])
    outs = pl.pallas_call(
        functools.partial(_ffn_body, final_norm=final_norm, cast_next=cast_next),
        grid=(steps,),
        in_specs=in_specs,
        out_specs=out_specs,
        out_shape=out_shape,
        scratch_shapes=[pltpu.VMEM((tile, D_FF), bf16)],
        compiler_params=_params("arbitrary"),
        name="ffn",
    )(*args)
    return outs[0], outs[1], (tuple(outs[2:]) if cast_next else None)


def _conv_glu(x, g_ref, wa_ref, wb_ref, ba_ref, bb_ref):
    h = _rms(x, g_ref[...]).astype(bf16)
    a = _dot(h, wa_ref[...]) + ba_ref[...]
    gate = _dot(h, wb_ref[...]) + bb_ref[...]
    return a * _sigmoid(gate)


def _conv_a_body(x_ref, g_ref, wa_ref, wb_ref, ba_ref, bb_ref, u_ref):
    u_ref[...] = _conv_glu(x_ref[...], g_ref, wa_ref, wb_ref, ba_ref, bb_ref)


def _conv_a(x, w, layer, j):
    rows = x.shape[0]
    tile = min(ROW_TILE, rows)
    return pl.pallas_call(
        _conv_a_body,
        grid=(rows // tile,),
        in_specs=[_rows(tile, D_MODEL), _vec(layer), _mat(j, D_MODEL, D_MODEL, 0), _mat(j, D_MODEL, D_MODEL, 1),
                  _vec(j, part=0), _vec(j, part=1)],
        out_specs=_rows(tile, D_MODEL),
        out_shape=jax.ShapeDtypeStruct((rows, D_MODEL), f32),
        compiler_params=_params("arbitrary"),
        name="conv_glu",
    )(x, w["norm_mix"], w["conv_w_pw1"], w["conv_w_pw1"], w["conv_b_pw1"], w["conv_b_pw1"])


def _conv_prompt_body(x_ref, g_ref, wa_ref, wb_ref, ba_ref, bb_ref, st_ref, w_ref, b_ref,
                      lg_ref, lb_ref, w2_ref, b2_ref, *rest, layer, first):
    o_ref, ns_ref, hist, cout, taps = rest[-5:]
    t = pl.program_id(1)
    tile = x_ref.shape[0]

    @pl.when(t == 0)
    def _():
        for l in range(N_SLABS):
            lanes = slice(l * LANES, (l + 1) * LANES)
            hist[l, CONV_HALO - CONV_STATE:CONV_HALO, :] = st_ref[0, :, lanes]
            for j in range(CONV_WIDTH):
                taps[l, j] = jnp.broadcast_to(w_ref[j:j + 1, lanes], (2 * SUBLANES, LANES)).astype(bf16)

    @pl.when(t > 0)
    def _():
        for l in range(N_SLABS):
            hist[l, 0:CONV_HALO, :] = hist[l, tile:tile + CONV_HALO, :]

    x = x_ref[...]
    u = _conv_glu(x, g_ref, wa_ref, wb_ref, ba_ref, bb_ref)
    first_tap = CONV_HALO - CONV_STATE
    for l in range(N_SLABS):
        lanes = slice(l * LANES, (l + 1) * LANES)
        hist[l, CONV_HALO:CONV_HALO + tile, :] = u[:, lanes]

        def block_pair(pair, carry, l=l):
            bases = [(pair * 2 + half) * CONV_BLOCK for half in range(2)]
            acc = [None] * CONV_STRIDE
            for shift in range(first_tap, first_tap + CONV_WIDTH + CONV_STRIDE - 1):
                rows = jnp.concatenate(
                    [hist[l, pl.ds(base + shift, SUBLANES, stride=CONV_STRIDE), :] for base in bases],
                    axis=0).astype(bf16)
                for t0 in range(CONV_STRIDE):
                    j = shift - t0 - first_tap
                    if 0 <= j < CONV_WIDTH:
                        term = rows.astype(f32) * taps[l, j].astype(f32)
                        acc[t0] = term if acc[t0] is None else acc[t0] + term
            for t0 in range(CONV_STRIDE):
                for half, base in enumerate(bases):
                    cout[l, pl.ds(base + t0, SUBLANES, stride=CONV_STRIDE), :] = (
                        acc[t0][half * SUBLANES:(half + 1) * SUBLANES])
            return carry

        lax.fori_loop(0, tile // (2 * CONV_BLOCK), block_pair, 0)

    c = jnp.concatenate([cout[l] for l in range(N_SLABS)], axis=1) + b_ref[...]
    o_ref[...] = _conv_tail(c, x, lg_ref, lb_ref, w2_ref, b2_ref)
    _own_layer(ns_ref, layer, first)[0] = jnp.concatenate(
        [hist[l, tile + CONV_HALO - CONV_STATE:tile + CONV_HALO, :] for l in range(N_SLABS)], axis=1)


def _conv_prompt(x, states, w, layer, j, prev, batch, seq):
    tile = CONV_TILE
    nt = seq // tile
    n_layers = states.shape[0]
    extra_specs, extra_args, aliases = _stacked(prev, 13, 1)
    seq_tile = pl.BlockSpec((tile, D_MODEL), lambda i, t: (i * nt + t, 0))
    state_block = pl.BlockSpec((None, 1, CONV_STATE, D_MODEL), lambda i, t: (j, i, 0, 0))
    return pl.pallas_call(
        functools.partial(_conv_prompt_body, layer=j, first=prev is None),
        grid=(batch, nt),
        in_specs=[seq_tile, _vec(layer), _mat(j, D_MODEL, D_MODEL, 0), _mat(j, D_MODEL, D_MODEL, 1),
                  _vec(j, part=0), _vec(j, part=1), state_block, _mat(j, CONV_WIDTH, D_MODEL), _vec(j),
                  _vec(j), _vec(j), _mat(j, D_MODEL, D_MODEL), _vec(j)] + extra_specs,
        out_specs=[seq_tile, _layer_block(prev is None, n_layers, j, (1, CONV_STATE, D_MODEL),
                                          lambda i, t: (i, 0, 0))],
        out_shape=[jax.ShapeDtypeStruct((batch * seq, D_MODEL), f32),
                   jax.ShapeDtypeStruct((n_layers, batch, CONV_STATE, D_MODEL), f32)],
        scratch_shapes=[pltpu.VMEM((N_SLABS, CONV_HALO + tile, LANES), f32),
                        pltpu.VMEM((N_SLABS, tile, LANES), f32),
                        pltpu.VMEM((N_SLABS, CONV_WIDTH, 2 * SUBLANES, LANES), bf16)],
        input_output_aliases=aliases,
        compiler_params=_params("arbitrary", "arbitrary"),
        name="conv_prompt",
    )(x, w["norm_mix"], w["conv_w_pw1"], w["conv_w_pw1"], w["conv_b_pw1"], w["conv_b_pw1"], states,
      w["conv_w_dw"], w["conv_b_dw"], w["conv_ln_g"], w["conv_ln_b"], w["conv_w_pw2"], w["conv_b_pw2"],
      *extra_args)


def _conv_b_sample_body(u_ref, st_ref, w_ref, b_ref, *rest, layer, first):
    c_ref, ns_ref = rest[-2:]
    ns = _own_layer(ns_ref, layer, first)
    steps = u_ref.shape[0]
    acc = [None] * steps
    for i in range(CONV_STATE + steps):
        row = st_ref[i] if i < CONV_STATE else u_ref[i - CONV_STATE]
        for t in range(steps):
            j = i - t
            if 0 <= j < CONV_WIDTH and i <= CONV_STATE + t:
                term = row * w_ref[j:j + 1, :]
                acc[t] = term if acc[t] is None else acc[t] + term
        if i >= steps:
            ns[i - steps] = row
    for t in range(steps):
        c_ref[t] = acc[t] + b_ref[...]


def _conv_b_sample(u3, states, w, j, prev):
    steps, batch, _ = u3.shape
    bb = CONV_SAMPLE_BATCH
    n_layers = states.shape[0]
    first = prev is None
    extra_specs, extra_args, aliases = _stacked(prev, 4, 1)
    step_block = pl.BlockSpec((steps, bb, D_MODEL), lambda i: (0, i, 0))
    return pl.pallas_call(
        functools.partial(_conv_b_sample_body, layer=j, first=first),
        grid=(batch // bb,),
        in_specs=[step_block, pl.BlockSpec((None, CONV_STATE, bb, D_MODEL), lambda i: (j, 0, i, 0)),
                  _mat(j, CONV_WIDTH, D_MODEL), _vec(j)] + extra_specs,
        out_specs=[step_block,
                   _layer_block(first, n_layers, j, (CONV_STATE, bb, D_MODEL), lambda i: (0, i, 0))],
        out_shape=[jax.ShapeDtypeStruct((steps, batch, D_MODEL), f32),
                   jax.ShapeDtypeStruct((n_layers, CONV_STATE, batch, D_MODEL), f32)],
        input_output_aliases=aliases,
        compiler_params=_params("arbitrary"),
        name="conv_dw_sample",
    )(u3, states, w["conv_w_dw"], w["conv_b_dw"], *extra_args)


def _conv_tail(c, x, lg_ref, lb_ref, w2_ref, b2_ref):
    d = c - jnp.mean(c, axis=-1, keepdims=True)
    var = jnp.mean(d * d, axis=-1, keepdims=True)
    y = _silu(d * lax.rsqrt(var + EPS) * lg_ref[...] + lb_ref[...])
    return x + _dot(y.astype(bf16), w2_ref[...]) + b2_ref[...]


def _conv_c_body(c_ref, x_ref, lg_ref, lb_ref, w2_ref, b2_ref, o_ref):
    o_ref[...] = _conv_tail(c_ref[...], x_ref[...], lg_ref, lb_ref, w2_ref, b2_ref)


def _conv_c(c, x, w, j):
    rows = x.shape[0]
    tile = min(ROW_TILE, rows)
    return pl.pallas_call(
        _conv_c_body,
        grid=(rows // tile,),
        in_specs=[_rows(tile, D_MODEL), _rows(tile, D_MODEL), _vec(j), _vec(j), _mat(j, D_MODEL, D_MODEL), _vec(j)],
        out_specs=_rows(tile, D_MODEL),
        out_shape=jax.ShapeDtypeStruct((rows, D_MODEL), f32),
        compiler_params=_params("arbitrary"),
        name="conv_out",
    )(c, x, w["conv_ln_g"], w["conv_ln_b"], w["conv_w_pw2"], w["conv_b_pw2"])


def _gla_a_body(x_ref, g_ref, wq_ref, wk_ref, wv_ref, wr_ref, wz_ref, wg2_ref, bg_ref,
                q_ref, k_ref, v_ref, r_ref, la_ref):
    q, k, v, r, la = _gla_project(x_ref[...], g_ref, wq_ref, wk_ref, wv_ref, wr_ref, wz_ref, wg2_ref, bg_ref)
    q_ref[...] = q
    k_ref[...] = k
    v_ref[...] = v
    r_ref[...] = r
    la_ref[...] = la


def _gla_project(x, g_ref, wq_ref, wk_ref, wv_ref, wr_ref, wz_ref, wg2_ref, bg_ref):
    h = _rms(x, g_ref[...]).astype(bf16)
    q = _dot(h, wq_ref[...]) * (GLA_DK ** -0.5)
    k = _dot(h, wk_ref[...])
    v = _dot(h, wv_ref[...])
    r = _dot(h, wr_ref[...])
    z = _dot(h, wz_ref[...])
    pre = _dot(z.astype(bf16), wg2_ref[...]) + bg_ref[...]
    la = (jnp.minimum(pre, 0.0) - jnp.log1p(jnp.exp(-jnp.abs(pre)))) * (1.0 / GLA_TAU)
    return q, k, v, r, la


def _gla_a(x, w, layer, j):
    rows = x.shape[0]
    tile = min(ROW_TILE, rows)
    widths = (GLA_DKT, GLA_DKT, GLA_DVT, GLA_DVT, GLA_DKT)
    return pl.pallas_call(
        _gla_a_body,
        grid=(rows // tile,),
        in_specs=[_rows(tile, D_MODEL), _vec(layer),
                  _mat(j, D_MODEL, GLA_DKT, 0), _mat(j, D_MODEL, GLA_DKT, 1),
                  _mat(j, D_MODEL, GLA_DVT, 1), _mat(j, D_MODEL, GLA_DVT, 2),
                  _mat(j, D_MODEL, LANES), _mat(j, LANES, GLA_DKT), _vec(j, GLA_DKT)],
        out_specs=[_rows(tile, n) for n in widths],
        out_shape=[jax.ShapeDtypeStruct((rows, n), f32) for n in widths],
        compiler_params=_params("arbitrary"),
        name="gla_proj",
    )(x, w["norm_mix"], w["gla_w_in"], w["gla_w_in"], w["gla_w_in"], w["gla_w_in"], w["gla_w_z"],
      w["gla_w_gate2"], w["gla_b_gate"])


def _split_bf16(x):
    hi = x.astype(bf16)
    return hi, (x - hi.astype(f32)).astype(bf16)


def _decay_columns(la_hi, la_lo, ones):
    total = _dot_t_lhs(la_hi, ones) + _dot_t_lhs(la_lo, ones)
    return jnp.concatenate([jnp.exp(total)] * (GLA_DV // LANES), axis=1)


def _gla_prompt_body(x_ref, g_ref, wq_ref, wk_ref, wv_ref, wr_ref, wz_ref, wg2_ref, bg_ref, s0_ref,
                     go_ref, wo_ref, *rest, layer, first):
    out_ref, so_ref, state, scores = rest[-4:]
    c = pl.program_id(1)
    seqs, chunk, _ = x_ref.shape

    @pl.when(c == 0)
    def _():
        state[...] = s0_ref[...]

    tri = (lax.broadcasted_iota(jnp.int32, (chunk, chunk), 1)
           <= lax.broadcasted_iota(jnp.int32, (chunk, chunk), 0)).astype(bf16)
    ones = jnp.ones((chunk, LANES), bf16)
    rows = slice(0, chunk)

    def projection(s):
        x = x_ref[s]
        h = _rms(x, g_ref[...]).astype(bf16)
        got = {"x": x}
        parts = {"q": [], "k": [], "v": [], "r": []}

        def gate():
            z = _dot(h, wz_ref[...])
            pre = _dot(z.astype(bf16), wg2_ref[...]) + bg_ref[...]
            got["la"] = (jnp.minimum(pre, 0.0) - jnp.log1p(jnp.exp(-jnp.abs(pre)))) * (1.0 / GLA_TAU)

        def column_tile(name, w_ref, c0, width, scale):
            def piece():
                out = _dot(h, w_ref[:, c0:c0 + MXU_COLS])
                parts[name].append(out * scale if scale is not None else out)
                if c0 + MXU_COLS == width:
                    got[name] = jnp.concatenate(parts[name], axis=1)
            return piece

        pieces = [gate]
        for name, w_ref, width, scale in (("q", wq_ref, GLA_DKT, GLA_DK ** -0.5), ("k", wk_ref, GLA_DKT, None),
                                          ("v", wv_ref, GLA_DVT, None), ("r", wr_ref, GLA_DVT, None)):
            pieces += [column_tile(name, w_ref, c0, width, scale) for c0 in range(0, width, MXU_COLS)]
        return got, pieces

    cur, pieces = projection(0)
    for piece in pieces:
        piece()
    for s in range(seqs):
        nxt, pieces = projection(s + 1) if s + 1 < seqs else (None, [])
        la_hi, la_lo = _split_bf16(cur["la"])
        gated = _gla_chunk(rows, cur["q"], cur["k"], cur["v"], cur["r"], la_hi, la_lo, tri, ones,
                           state.at[s], scores.at[s], go_ref, pieces)
        out_ref[s] = cur["x"] + _dot(gated, wo_ref[...])
        cur = nxt

    @pl.when(c == pl.num_programs(1) - 1)
    def _():
        _own_layer(so_ref, layer, first)[...] = state[...]


def _gla_chunk(rows, q, k, v, r, la_hi, la_lo, tri, ones, state, scores, go_ref, between=()):
    chunk = tri.shape[0]
    blocks = chunk // GLA_SUB
    slots = GLA_HEADS * blocks
    emit_at = {}
    for n, piece in enumerate(between):
        emit_at.setdefault(n * slots // len(between), []).append(piece)
    la_hi, la_lo = la_hi[rows], la_lo[rows]
    b_all = _dot(tri, la_hi) + _dot(tri, la_lo)
    gated = []
    for h in range(GLA_HEADS):
        ks = slice(h * GLA_DK, (h + 1) * GLA_DK)
        vs = slice(h * GLA_DV, (h + 1) * GLA_DV)
        b = b_all[:, ks]
        qh = q[rows, ks]
        kh = k[rows, ks]
        vb = v[rows, vs].astype(bf16)
        b_last = b[chunk - 1:chunk, :]
        q_dec = (qh * jnp.exp(b)).astype(bf16)
        k_end = (kh * jnp.exp(b_last - b)).astype(bf16)
        s_old = state[h]
        o_inter = _dot(q_dec, s_old.astype(bf16))

        for i in range(chunk // GLA_SUB):
            r0 = i * GLA_SUB
            nk = r0 + GLA_SUB
            npad = LANES * (-(-nk // LANES))
            bq = b[r0:nk]
            bk = b[0:nk]
            if i == 0:
                q_exp, k_exp = bq, -bk
            else:
                ref = b[r0 - 1:r0, :]
                q_exp, k_exp = bq - ref, ref - bk
            qi = (qh[r0:nk] * jnp.exp(q_exp)).astype(bf16)
            ki = (kh[0:nk] * jnp.exp(k_exp)).astype(bf16)
            if nk < npad:
                ki = jnp.concatenate([ki, jnp.zeros((npad - nk, GLA_DK), bf16)], axis=0)
            s = _dot_t_rhs(qi, ki)
            causal = (lax.broadcasted_iota(jnp.int32, (GLA_SUB, npad), 1)
                      <= lax.broadcasted_iota(jnp.int32, (GLA_SUB, npad), 0) + r0)
            scores[h, r0:nk, 0:npad] = jnp.where(causal, s, 0.0).astype(bf16)
            if npad < chunk:
                scores[h, r0:nk, npad:chunk] = jnp.zeros((GLA_SUB, chunk - npad), bf16)
            for piece in emit_at.get(h * blocks + i, ()):
                piece()

        o = o_inter + _dot(scores[h], vb)
        gated.append(_gla_gate(o, r[rows, vs], go_ref))
        state[h] = s_old * _decay_columns(la_hi[:, ks], la_lo[:, ks], ones) + _dot_t_lhs(k_end, vb)
    return jnp.concatenate(gated, axis=1)


def _gla_prompt(x, states, w, layer, j, prev, batch, seq):
    chunk, seqs = GLA_CHUNK, GLA_SEQS
    n_layers = states.shape[0]
    first = prev is None
    extra_specs, extra_args, aliases = _stacked(prev, 12, 1)
    seq_tile = pl.BlockSpec((seqs, chunk, D_MODEL), lambda i, c: (i, c, 0))
    state_shape = (seqs, GLA_HEADS, GLA_DK, GLA_DV)
    out, new_states = pl.pallas_call(
        functools.partial(_gla_prompt_body, layer=j, first=first),
        grid=(batch // seqs, seq // chunk),
        in_specs=[seq_tile, _vec(layer),
                  _mat(j, D_MODEL, GLA_DKT, 0), _mat(j, D_MODEL, GLA_DKT, 1),
                  _mat(j, D_MODEL, GLA_DVT, 1), _mat(j, D_MODEL, GLA_DVT, 2),
                  _mat(j, D_MODEL, LANES), _mat(j, LANES, GLA_DKT), _vec(j, GLA_DKT),
                  pl.BlockSpec((None,) + state_shape, lambda i, c: (j, i, 0, 0, 0)),
                  _vec(j, GLA_DV), _mat(j, GLA_DVT, D_MODEL)] + extra_specs,
        out_specs=[seq_tile, _layer_block(first, n_layers, j, state_shape, lambda i, c: (i, 0, 0, 0))],
        out_shape=[jax.ShapeDtypeStruct((batch, seq, D_MODEL), f32),
                   jax.ShapeDtypeStruct((n_layers, batch, GLA_HEADS, GLA_DK, GLA_DV), f32)],
        scratch_shapes=[pltpu.VMEM(state_shape, f32), pltpu.VMEM((seqs, GLA_HEADS, chunk, chunk), bf16)],
        input_output_aliases=aliases,
        compiler_params=_params("arbitrary", "arbitrary"),
        name="gla_prompt",
    )(x.reshape(batch, seq, D_MODEL), w["norm_mix"], w["gla_w_in"], w["gla_w_in"], w["gla_w_in"], w["gla_w_in"],
      w["gla_w_z"], w["gla_w_gate2"], w["gla_b_gate"], states, w["gla_onorm_g"], w["gla_w_out"], *extra_args)
    return out.reshape(batch * seq, D_MODEL), new_states


def _gla_b_sample_body(q_ref, k_ref, v_ref, la_ref, s_hbm, *rest, layer, first):
    o_ref, so_ref, ring, sems = rest[-4:]
    so = _own_layer(so_ref, layer, first)
    steps, bb, _ = q_ref.shape

    step, n_steps = pl.program_id(0), pl.num_programs(0)

    def fetch(t, slot):
        return pltpu.make_async_copy(s_hbm.at[layer, pl.ds(t * bb, bb)], ring.at[slot], sems.at[slot])

    @pl.when(step == 0)
    def _():
        for t in range(GLA_RING - 1):
            @pl.when(t < n_steps)
            def _():
                fetch(t, t).start()

    ahead = step + (GLA_RING - 1)

    @pl.when(ahead < n_steps)
    def _():
        fetch(ahead, ahead % GLA_RING).start()

    slot = step % GLA_RING
    fetch(step, slot).wait()
    s0_ref = ring.at[slot]
    causal = (lax.broadcasted_iota(jnp.int32, (steps, steps), 1)
              <= lax.broadcasted_iota(jnp.int32, (steps, steps), 0))
    ones = jnp.ones((steps, LANES), bf16)

    for i in range(bb):
        la = la_ref[:, i, :]
        q, k, v = q_ref[:, i, :], k_ref[:, i, :], v_ref[:, i, :]
        cum = [la[0:1]]
        for t in range(1, steps):
            cum.append(cum[-1] + la[t:t + 1])
        b_all = jnp.concatenate(cum, axis=0)
        la_hi, la_lo = _split_bf16(la)
        heads = []
        for h in range(GLA_HEADS):
            ks = slice(h * GLA_DK, (h + 1) * GLA_DK)
            vs = slice(h * GLA_DV, (h + 1) * GLA_DV)
            b = b_all[:, ks]
            qh = q[:, ks]
            kh = k[:, ks]
            vb = v[:, vs].astype(bf16)
            b_last = b[steps - 1:steps, :]
            q_dec = (qh * jnp.exp(b)).astype(bf16)
            k_inv = (kh * jnp.exp(-b)).astype(bf16)
            k_end = (kh * jnp.exp(b_last - b)).astype(bf16)
            s_old = s0_ref[i, h]
            sc = jnp.where(causal, _dot_t_rhs(q_dec, k_inv), 0.0).astype(bf16)
            heads.append(_dot(sc, vb) + _dot(q_dec, s_old.astype(bf16)))
            so[i, h] = s_old * _decay_columns(la_hi[:, ks], la_lo[:, ks], ones) + _dot_t_lhs(k_end, vb)
        o_ref[:, i, :] = jnp.concatenate(heads, axis=1)


def _gla_b_sample(q, k, v, la, states, j, prev):
    steps, batch, _ = q.shape
    bb = GLA_SAMPLE_BATCH
    n_layers = states.shape[0]
    first = prev is None
    extra_specs, extra_args, aliases = _stacked(prev, 5, 1)
    step_block = lambda n: pl.BlockSpec((steps, bb, n), lambda i: (0, i, 0))
    state_shape = (bb, GLA_HEADS, GLA_DK, GLA_DV)
    return pl.pallas_call(
        functools.partial(_gla_b_sample_body, layer=j, first=first),
        grid=(batch // bb,),
        in_specs=[step_block(GLA_DKT), step_block(GLA_DKT), step_block(GLA_DVT), step_block(GLA_DKT),
                  pl.BlockSpec(memory_space=pl.ANY)] + extra_specs,
        out_specs=[step_block(GLA_DVT),
                   _layer_block(first, n_layers, j, state_shape, lambda i: (i, 0, 0, 0))],
        out_shape=[jax.ShapeDtypeStruct((steps, batch, GLA_DVT), f32),
                   jax.ShapeDtypeStruct((n_layers, batch, GLA_HEADS, GLA_DK, GLA_DV), f32)],
        scratch_shapes=[pltpu.VMEM((GLA_RING,) + state_shape, f32), pltpu.SemaphoreType.DMA((GLA_RING,))],
        input_output_aliases=aliases,
        compiler_params=_params("arbitrary"),
        name="gla_core_sample",
    )(q, k, v, la, states, *extra_args)


def _gla_gate(o, r, go_ref):
    return (_rms(o, go_ref[...]) * _silu(r)).astype(bf16)


def _gla_c_body(o_ref, r_ref, x_ref, go_ref, wo_ref, out_ref):
    parts = []
    for h in range(GLA_HEADS):
        vs = slice(h * GLA_DV, (h + 1) * GLA_DV)
        parts.append(_gla_gate(o_ref[:, vs], r_ref[:, vs], go_ref))
    out_ref[...] = x_ref[...] + _dot(jnp.concatenate(parts, axis=1), wo_ref[...])


def _gla_c(o, r, x, w, j):
    rows = x.shape[0]
    tile = min(ROW_TILE, rows)
    return pl.pallas_call(
        _gla_c_body,
        grid=(rows // tile,),
        in_specs=[_rows(tile, GLA_DVT), _rows(tile, GLA_DVT), _rows(tile, D_MODEL), _vec(j, GLA_DV),
                  _mat(j, GLA_DVT, D_MODEL)],
        out_specs=_rows(tile, D_MODEL),
        out_shape=jax.ShapeDtypeStruct((rows, D_MODEL), f32),
        compiler_params=_params("arbitrary"),
        name="gla_out",
    )(o, r, x, w["gla_onorm_g"], w["gla_w_out"])


def _mix_sample(x, conv_state, gla_state, w, i, new_conv, new_gla, batch, steps):
    j = i // 2
    if i % 2 == 0:
        u = _conv_a(x, w, i, j)
        c, new_conv = _conv_b_sample(u.reshape(steps, batch, D_MODEL), conv_state, w, j, new_conv)
        x = _conv_c(c.reshape(steps * batch, D_MODEL), x, w, j)
    else:
        q, k, v, r, la = _gla_a(x, w, i, j)
        by_step = lambda a: a.reshape(steps, batch, a.shape[-1])
        o, new_gla = _gla_b_sample(by_step(q), by_step(k), by_step(v), by_step(la), gla_state, j, new_gla)
        x = _gla_c(o.reshape(steps * batch, GLA_DVT), r, x, w, j)
    return x, new_conv, new_gla


def _mix_prompt(x, conv_state, gla_state, w, i, new_conv, new_gla, batch, seq):
    j = i // 2
    if i % 2 == 0:
        x, new_conv = _conv_prompt(x, conv_state, w, i, j, new_conv, batch, seq)
    else:
        x, new_gla = _gla_prompt(x, gla_state, w, i, j, new_gla, batch, seq)
    return x, new_conv, new_gla


def kernel(x_prompt, x_sample, state_conv, state_gla, norm_ffn_pre, norm_mix, norm_ffn_post, norm_final,
           ffn_w_gate, ffn_w_up, ffn_w_down, conv_w_pw1, conv_b_pw1, conv_w_dw, conv_b_dw, conv_ln_g, conv_ln_b,
           conv_w_pw2, conv_b_pw2, gla_w_in, gla_w_gate2, gla_b_gate, gla_onorm_g, gla_w_out):
    row = lambda a: a[..., None, :]
    z_start = 2 * GLA_DKT + 2 * GLA_DVT
    w_z = jnp.pad(gla_w_in[:, :, z_start:], ((0, 0), (0, 0), (0, LANES - GLA_GATE_RANK)))
    w_gate2 = jnp.pad(gla_w_gate2, ((0, 0), (0, LANES - GLA_GATE_RANK), (0, 0)))
    w = dict(
        norm_ffn_pre=row(norm_ffn_pre), norm_mix=row(norm_mix), norm_ffn_post=row(norm_ffn_post),
        norm_final=norm_final[None, :],
        ffn_w_gate=ffn_w_gate, ffn_w_up=ffn_w_up, ffn_w_down=ffn_w_down,
        conv_w_pw1=conv_w_pw1.astype(bf16), conv_b_pw1=row(conv_b_pw1),
        conv_w_dw=conv_w_dw, conv_b_dw=row(conv_b_dw), conv_ln_g=row(conv_ln_g), conv_ln_b=row(conv_ln_b),
        conv_w_pw2=conv_w_pw2.astype(bf16), conv_b_pw2=row(conv_b_pw2),
        gla_w_in=gla_w_in.astype(bf16), gla_w_z=w_z.astype(bf16), gla_w_gate2=w_gate2.astype(bf16),
        gla_b_gate=row(gla_b_gate), gla_onorm_g=row(gla_onorm_g), gla_w_out=gla_w_out.astype(bf16),
    )
    batch, seq, _ = x_prompt.shape
    dec_batch, steps, _ = x_sample.shape
    conv0 = jnp.zeros((state_conv.shape[0], batch) + state_conv.shape[2:], x_prompt.dtype)
    gla0 = jnp.zeros((state_gla.shape[0], batch) + state_gla.shape[2:], x_prompt.dtype)
    conv_in_s = state_conv.transpose(0, 2, 1, 3)
    xp = x_prompt.reshape(batch * seq, D_MODEL)
    xs = x_sample.transpose(1, 0, 2).reshape(steps * dec_batch, D_MODEL)

    wb = tuple(a[0, 0].astype(bf16) for a in (ffn_w_gate, ffn_w_up, ffn_w_down))
    conv_p = gla_p = conv_s = gla_s = None
    for i in range(DEPTH):
        xp, xs, wb = _ffn(xp, xs, wb, w, "norm_ffn_pre", i, (i, 1))
        xp, conv_p, gla_p = _mix_prompt(xp, conv0, gla0, w, i, conv_p, gla_p, batch, seq)
        xs, conv_s, gla_s = _mix_sample(xs, conv_in_s, state_gla, w, i, conv_s, gla_s, dec_batch, steps)
        last = i == DEPTH - 1
        xp, xs, wb = _ffn(xp, xs, wb, w, "norm_ffn_post", i, None if last else (i + 1, 0), final_norm=last)

    y_p = xp.reshape(batch, seq, D_MODEL)
    y_s = xs.reshape(steps, dec_batch, D_MODEL).transpose(1, 0, 2)
    return (y_p, y_s, conv_p, gla_p, conv_s.transpose(0, 2, 1, 3), gla_s)
```

```python
import functools

import jax
import jax.numpy as jnp
from jax import lax
from jax.experimental import pallas as pl
from jax.experimental.pallas import tpu as pltpu

f32 = jnp.float32
bf16 = jnp.bfloat16

D_MODEL = 1024
D_FF = 2816
DEPTH = 4
CONV_WIDTH = 31
CONV_STATE = CONV_WIDTH - 1
GLA_HEADS = 4
GLA_DK = 128
GLA_DV = 256
GLA_DKT = GLA_HEADS * GLA_DK
GLA_DVT = GLA_HEADS * GLA_DV
GLA_GATE_RANK = 16
GLA_TAU = 16.0
EPS = 1e-6

LANES = 128
SUBLANES = 8
N_SLABS = D_MODEL // LANES
MXU_COLS = 256
VMEM_LIMIT = 56 * 1024 * 1024

ROW_TILE = 512
FFN_TILE = 1024
FFN_CHUNK = 256
CONV_TILE = 1024
CONV_HALO = 32
CONV_STRIDE = 4
CONV_BLOCK = SUBLANES * CONV_STRIDE
CONV_SAMPLE_BATCH = 32
GLA_CHUNK = 256
GLA_SEQS = 2
GLA_SUB = 32
GLA_SAMPLE_BATCH = 8
GLA_RING = 3


def _params(*semantics):
    return pltpu.CompilerParams(dimension_semantics=semantics, vmem_limit_bytes=VMEM_LIMIT)


def _resident(block, index=None):
    index = (0,) * len(block) if index is None else index
    return pl.BlockSpec(block, lambda *_: index, pipeline_mode=pl.Buffered(1))


def _vec(layer, width=D_MODEL, part=0):
    return _resident((None, 1, width), (layer, 0, part))


def _mat(layer, rows, cols, part=0):
    return _resident((None, rows, cols), (layer, 0, part))


def _rows(tile, width):
    return pl.BlockSpec((tile, width), lambda i: (i, 0))


def _stacked(prev, in_count, out_index):
    if prev is None:
        return [], [], {}
    return [pl.BlockSpec(memory_space=pl.ANY)], [prev], {in_count: out_index}


def _layer_block(first, n_layers, layer, block, index):
    lead, at = (n_layers, 0) if first else (None, layer)
    return pl.BlockSpec((lead,) + block, lambda *grid: (at,) + index(*grid))


def _own_layer(ref, layer, first):
    if not first:
        return ref
    for other in range(ref.shape[0]):
        if other != layer:
            ref[other] = jnp.zeros(ref.shape[1:], ref.dtype)
    return ref.at[layer]


def _rms(x, g):
    return x * lax.rsqrt(jnp.mean(x * x, axis=-1, keepdims=True) + EPS) * g


def _sigmoid(x):
    return 0.5 * jnp.tanh(0.5 * x) + 0.5


def _silu(x):
    h = 0.5 * x
    return h + h * jnp.tanh(h)


def _dot(a, b):
    return jnp.dot(a, b, preferred_element_type=f32)


def _dot_t_rhs(a, b):
    return lax.dot_general(a, b, (((1,), (1,)), ((), ())), preferred_element_type=f32)


def _dot_t_lhs(a, b):
    return lax.dot_general(a, b, (((0,), (0,)), ((), ())), preferred_element_type=f32)


def _ffn_rows(x, g_ref, wg_ref, wu_ref, wd_ref, gf_ref, a_scr, final_norm):
    rows = x.shape[0]
    h = _rms(x, g_ref[...]).astype(bf16)
    for j in range(D_FF // FFN_CHUNK):
        cols = pl.ds(j * FFN_CHUNK, FFN_CHUNK)
        gate = _dot(h, wg_ref[:, cols])
        up = _dot(h, wu_ref[:, cols])
        a_scr[0:rows, cols] = (_silu(gate) * up).astype(bf16)
    y = x + 0.5 * _dot(a_scr[0:rows, :], wd_ref[...])
    return _rms(y, gf_ref[...]) if final_norm else y


def _ffn_tile(x_ref, o_ref, g_ref, wg_ref, wu_ref, wd_ref, gf_ref, a_scr, final_norm):
    half = x_ref.shape[0] // 2
    lo, hi = slice(0, half), slice(half, 2 * half)
    chunks = [pl.ds(j * FFN_CHUNK, FFN_CHUNK) for j in range(D_FF // FFN_CHUNK)]

    def hidden(h, rows, cols):
        a_scr[rows, cols] = (_silu(_dot(h, wg_ref[:, cols])) * _dot(h, wu_ref[:, cols])).astype(bf16)

    def finish(x, rows):
        y = x + 0.5 * _dot(a_scr[rows, :], wd_ref[...])
        o_ref[rows, :] = _rms(y, gf_ref[...]) if final_norm else y

    x_lo = x_ref[lo, :]
    h_lo = _rms(x_lo, g_ref[...]).astype(bf16)
    hidden(h_lo, lo, chunks[0])
    x_hi = x_ref[hi, :]
    h_hi = _rms(x_hi, g_ref[...]).astype(bf16)
    for cols in chunks[1:]:
        hidden(h_lo, lo, cols)
    hidden(h_hi, hi, chunks[0])
    finish(x_lo, lo)
    for cols in chunks[1:]:
        hidden(h_hi, hi, cols)
    finish(x_hi, hi)


def _ffn_body(xp_ref, xs_ref, g_ref, wg_ref, wu_ref, wd_ref, gf_ref, *rest, final_norm, cast_next):
    op_ref, os_ref = rest[3:5] if cast_next else rest[0:2]
    a_scr = rest[-1]
    weights = (g_ref, wg_ref, wu_ref, wd_ref, gf_ref, a_scr, final_norm)
    _ffn_tile(xp_ref, op_ref, *weights)

    @pl.when(pl.program_id(0) == pl.num_programs(0) - 1)
    def _():
        os_ref[...] = _ffn_rows(xs_ref[...], *weights)

    if cast_next:
        for src, dst in zip(rest[0:3], rest[5:8]):
            dst[...] = src[...].astype(bf16)


def _ffn(xp, xs, wb, w, gain, layer, nxt, *, final_norm=False):
    rows = xp.shape[0]
    tile = FFN_TILE
    steps = rows // tile
    cast_next = nxt is not None
    in_specs = [_rows(tile, D_MODEL), _resident(xs.shape), _vec(layer), _resident((D_MODEL, D_FF)),
                _resident((D_MODEL, D_FF)), _resident((D_FF, D_MODEL)), _resident((1, D_MODEL))]
    out_specs = [_rows(tile, D_MODEL), pl.BlockSpec(xs.shape, lambda i: (0, 0))]
    out_shape = [jax.ShapeDtypeStruct(xp.shape, f32), jax.ShapeDtypeStruct(xs.shape, f32)]
    args = [xp, xs, w[gain], *wb, w["norm_final"]]
    if cast_next:
        for name, (r, c) in (("ffn_w_gate", (D_MODEL, D_FF)), ("ffn_w_up", (D_MODEL, D_FF)),
                             ("ffn_w_down", (D_FF, D_MODEL))):
            in_specs.append(pl.BlockSpec((None, None, r // steps, c), lambda i: nxt + (i, 0)))
            out_specs.append(pl.BlockSpec((r // steps, c), lambda i: (i, 0)))
            out_shape.append(jax.ShapeDtypeStruct((r, c), bf16))
            args.append(w[name])
    outs = pl.pallas_call(
        functools.partial(_ffn_body, final_norm=final_norm, cast_next=cast_next),
        grid=(steps,),
        in_specs=in_specs,
        out_specs=out_specs,
        out_shape=out_shape,
        scratch_shapes=[pltpu.VMEM((tile, D_FF), bf16)],
        compiler_params=_params("arbitrary"),
        name="ffn",
    )(*args)
    return outs[0], outs[1], (tuple(outs[2:]) if cast_next else None)


def _conv_glu(x, g_ref, wa_ref, wb_ref, ba_ref, bb_ref):
    h = _rms(x, g_ref[...]).astype(bf16)
    a = _dot(h, wa_ref[...]) + ba_ref[...]
    gate = _dot(h, wb_ref[...]) + bb_ref[...]
    return a * _sigmoid(gate)


def _conv_a_body(x_ref, g_ref, wa_ref, wb_ref, ba_ref, bb_ref, u_ref):
    u_ref[...] = _conv_glu(x_ref[...], g_ref, wa_ref, wb_ref, ba_ref, bb_ref)


def _conv_a(x, w, layer, j):
    rows = x.shape[0]
    tile = min(ROW_TILE, rows)
    return pl.pallas_call(
        _conv_a_body,
        grid=(rows // tile,),
        in_specs=[_rows(tile, D_MODEL), _vec(layer), _mat(j, D_MODEL, D_MODEL, 0), _mat(j, D_MODEL, D_MODEL, 1),
                  _vec(j, part=0), _vec(j, part=1)],
        out_specs=_rows(tile, D_MODEL),
        out_shape=jax.ShapeDtypeStruct((rows, D_MODEL), f32),
        compiler_params=_params("arbitrary"),
        name="conv_glu",
    )(x, w["norm_mix"], w["conv_w_pw1"], w["conv_w_pw1"], w["conv_b_pw1"], w["conv_b_pw1"])


def _conv_prompt_body(x_ref, g_ref, wa_ref, wb_ref, ba_ref, bb_ref, st_ref, w_ref, b_ref,
                      lg_ref, lb_ref, w2_ref, b2_ref, *rest, layer, first):
    o_ref, ns_ref, hist, cout, taps = rest[-5:]
    t = pl.program_id(1)
    tile = x_ref.shape[0]

    @pl.when(t == 0)
    def _():
        for l in range(N_SLABS):
            lanes = slice(l * LANES, (l + 1) * LANES)
            hist[l, CONV_HALO - CONV_STATE:CONV_HALO, :] = st_ref[0, :, lanes]
            for j in range(CONV_WIDTH):
                taps[l, j] = jnp.broadcast_to(w_ref[j:j + 1, lanes], (2 * SUBLANES, LANES)).astype(bf16)

    @pl.when(t > 0)
    def _():
        for l in range(N_SLABS):
            hist[l, 0:CONV_HALO, :] = hist[l, tile:tile + CONV_HALO, :]

    x = x_ref[...]
    u = _conv_glu(x, g_ref, wa_ref, wb_ref, ba_ref, bb_ref)
    first_tap = CONV_HALO - CONV_STATE
    for l in range(N_SLABS):
        lanes = slice(l * LANES, (l + 1) * LANES)
        hist[l, CONV_HALO:CONV_HALO + tile, :] = u[:, lanes]

        def block_pair(pair, carry, l=l):
            bases = [(pair * 2 + half) * CONV_BLOCK for half in range(2)]
            acc = [None] * CONV_STRIDE
            for shift in range(first_tap, first_tap + CONV_WIDTH + CONV_STRIDE - 1):
                rows = jnp.concatenate(
                    [hist[l, pl.ds(base + shift, SUBLANES, stride=CONV_STRIDE), :] for base in bases],
                    axis=0).astype(bf16)
                for t0 in range(CONV_STRIDE):
                    j = shift - t0 - first_tap
                    if 0 <= j < CONV_WIDTH:
                        term = rows.astype(f32) * taps[l, j].astype(f32)
                        acc[t0] = term if acc[t0] is None else acc[t0] + term
            for t0 in range(CONV_STRIDE):
                for half, base in enumerate(bases):
                    cout[l, pl.ds(base + t0, SUBLANES, stride=CONV_STRIDE), :] = (
                        acc[t0][half * SUBLANES:(half + 1) * SUBLANES])
            return carry

        lax.fori_loop(0, tile // (2 * CONV_BLOCK), block_pair, 0)

    c = jnp.concatenate([cout[l] for l in range(N_SLABS)], axis=1) + b_ref[...]
    o_ref[...] = _conv_tail(c, x, lg_ref, lb_ref, w2_ref, b2_ref)
    _own_layer(ns_ref, layer, first)[0] = jnp.concatenate(
        [hist[l, tile + CONV_HALO - CONV_STATE:tile + CONV_HALO, :] for l in range(N_SLABS)], axis=1)


def _conv_prompt(x, states, w, layer, j, prev, batch, seq):
    tile = CONV_TILE
    nt = seq // tile
    n_layers = states.shape[0]
    extra_specs, extra_args, aliases = _stacked(prev, 13, 1)
    seq_tile = pl.BlockSpec((tile, D_MODEL), lambda i, t: (i * nt + t, 0))
    state_block = pl.BlockSpec((None, 1, CONV_STATE, D_MODEL), lambda i, t: (j, i, 0, 0))
    return pl.pallas_call(
        functools.partial(_conv_prompt_body, layer=j, first=prev is None),
        grid=(batch, nt),
        in_specs=[seq_tile, _vec(layer), _mat(j, D_MODEL, D_MODEL, 0), _mat(j, D_MODEL, D_MODEL, 1),
                  _vec(j, part=0), _vec(j, part=1), state_block, _mat(j, CONV_WIDTH, D_MODEL), _vec(j),
                  _vec(j), _vec(j), _mat(j, D_MODEL, D_MODEL), _vec(j)] + extra_specs,
        out_specs=[seq_tile, _layer_block(prev is None, n_layers, j, (1, CONV_STATE, D_MODEL),
                                          lambda i, t: (i, 0, 0))],
        out_shape=[jax.ShapeDtypeStruct((batch * seq, D_MODEL), f32),
                   jax.ShapeDtypeStruct((n_layers, batch, CONV_STATE, D_MODEL), f32)],
        scratch_shapes=[pltpu.VMEM((N_SLABS, CONV_HALO + tile, LANES), f32),
                        pltpu.VMEM((N_SLABS, tile, LANES), f32),
                        pltpu.VMEM((N_SLABS, CONV_WIDTH, 2 * SUBLANES, LANES), bf16)],
        input_output_aliases=aliases,
        compiler_params=_params("arbitrary", "arbitrary"),
        name="conv_prompt",
    )(x, w["norm_mix"], w["conv_w_pw1"], w["conv_w_pw1"], w["conv_b_pw1"], w["conv_b_pw1"], states,
      w["conv_w_dw"], w["conv_b_dw"], w["conv_ln_g"], w["conv_ln_b"], w["conv_w_pw2"], w["conv_b_pw2"],
      *extra_args)


def _conv_b_sample_body(u_ref, st_ref, w_ref, b_ref, *rest, layer, first):
    c_ref, ns_ref = rest[-2:]
    ns = _own_layer(ns_ref, layer, first)
    steps = u_ref.shape[0]
    acc = [None] * steps
    for i in range(CONV_STATE + steps):
        row = st_ref[i] if i < CONV_STATE else u_ref[i - CONV_STATE]
        for t in range(steps):
            j = i - t
            if 0 <= j < CONV_WIDTH and i <= CONV_STATE + t:
                term = row * w_ref[j:j + 1, :]
                acc[t] = term if acc[t] is None else acc[t] + term
        if i >= steps:
            ns[i - steps] = row
    for t in range(steps):
        c_ref[t] = acc[t] + b_ref[...]


def _conv_b_sample(u3, states, w, j, prev):
    steps, batch, _ = u3.shape
    bb = min(CONV_SAMPLE_BATCH, batch)
    n_layers = states.shape[0]
    first = prev is None
    extra_specs, extra_args, aliases = _stacked(prev, 4, 1)
    step_block = pl.BlockSpec((steps, bb, D_MODEL), lambda i: (0, i, 0))
    return pl.pallas_call(
        functools.partial(_conv_b_sample_body, layer=j, first=first),
        grid=(batch // bb,),
        in_specs=[step_block, pl.BlockSpec((None, CONV_STATE, bb, D_MODEL), lambda i: (j, 0, i, 0)),
                  _mat(j, CONV_WIDTH, D_MODEL), _vec(j)] + extra_specs,
        out_specs=[step_block,
                   _layer_block(first, n_layers, j, (CONV_STATE, bb, D_MODEL), lambda i: (0, i, 0))],
        out_shape=[jax.ShapeDtypeStruct((steps, batch, D_MODEL), f32),
                   jax.ShapeDtypeStruct((n_layers, CONV_STATE, batch, D_MODEL), f32)],
        input_output_aliases=aliases,
        compiler_params=_params("arbitrary"),
        name="conv_dw_sample",
    )(u3, states, w["conv_w_dw"], w["conv_b_dw"], *extra_args)


def _conv_tail(c, x, lg_ref, lb_ref, w2_ref, b2_ref):
    d = c - jnp.mean(c, axis=-1, keepdims=True)
    var = jnp.mean(d * d, axis=-1, keepdims=True)
    y = _silu(d * lax.rsqrt(var + EPS) * lg_ref[...] + lb_ref[...])
    return x + _dot(y.astype(bf16), w2_ref[...]) + b2_ref[...]


def _conv_c_body(c_ref, x_ref, lg_ref, lb_ref, w2_ref, b2_ref, o_ref):
    o_ref[...] = _conv_tail(c_ref[...], x_ref[...], lg_ref, lb_ref, w2_ref, b2_ref)


def _conv_c(c, x, w, j):
    rows = x.shape[0]
    tile = min(ROW_TILE, rows)
    return pl.pallas_call(
        _conv_c_body,
        grid=(rows // tile,),
        in_specs=[_rows(tile, D_MODEL), _rows(tile, D_MODEL), _vec(j), _vec(j), _mat(j, D_MODEL, D_MODEL), _vec(j)],
        out_specs=_rows(tile, D_MODEL),
        out_shape=jax.ShapeDtypeStruct((rows, D_MODEL), f32),
        compiler_params=_params("arbitrary"),
        name="conv_out",
    )(c, x, w["conv_ln_g"], w["conv_ln_b"], w["conv_w_pw2"], w["conv_b_pw2"])


def _gla_a_body(x_ref, g_ref, wq_ref, wk_ref, wv_ref, wr_ref, wz_ref, wg2_ref, bg_ref,
                q_ref, k_ref, v_ref, r_ref, la_ref):
    q, k, v, r, la = _gla_project(x_ref[...], g_ref, wq_ref, wk_ref, wv_ref, wr_ref, wz_ref, wg2_ref, bg_ref)
    q_ref[...] = q
    k_ref[...] = k
    v_ref[...] = v
    r_ref[...] = r
    la_ref[...] = la


def _gla_project(x, g_ref, wq_ref, wk_ref, wv_ref, wr_ref, wz_ref, wg2_ref, bg_ref):
    h = _rms(x, g_ref[...]).astype(bf16)
    q = _dot(h, wq_ref[...]) * (GLA_DK ** -0.5)
    k = _dot(h, wk_ref[...])
    v = _dot(h, wv_ref[...])
    r = _dot(h, wr_ref[...])
    z = _dot(h, wz_ref[...])
    pre = _dot(z.astype(bf16), wg2_ref[...]) + bg_ref[...]
    la = (jnp.minimum(pre, 0.0) - jnp.log1p(jnp.exp(-jnp.abs(pre)))) * (1.0 / GLA_TAU)
    return q, k, v, r, la


def _gla_a(x, w, layer, j):
    rows = x.shape[0]
    tile = min(ROW_TILE, rows)
    widths = (GLA_DKT, GLA_DKT, GLA_DVT, GLA_DVT, GLA_DKT)
    return pl.pallas_call(
        _gla_a_body,
        grid=(rows // tile,),
        in_specs=[_rows(tile, D_MODEL), _vec(layer),
                  _mat(j, D_MODEL, GLA_DKT, 0), _mat(j, D_MODEL, GLA_DKT, 1),
                  _mat(j, D_MODEL, GLA_DVT, 1), _mat(j, D_MODEL, GLA_DVT, 2),
                  _mat(j, D_MODEL, LANES), _mat(j, LANES, GLA_DKT), _vec(j, GLA_DKT)],
        out_specs=[_rows(tile, n) for n in widths],
        out_shape=[jax.ShapeDtypeStruct((rows, n), f32) for n in widths],
        compiler_params=_params("arbitrary"),
        name="gla_proj",
    )(x, w["norm_mix"], w["gla_w_in"], w["gla_w_in"], w["gla_w_in"], w["gla_w_in"], w["gla_w_z"],
      w["gla_w_gate2"], w["gla_b_gate"])


def _split_bf16(x):
    hi = x.astype(bf16)
    return hi, (x - hi.astype(f32)).astype(bf16)


def _decay_columns(la_hi, la_lo, ones):
    total = _dot_t_lhs(la_hi, ones) + _dot_t_lhs(la_lo, ones)
    return jnp.concatenate([jnp.exp(total)] * (GLA_DV // LANES), axis=1)


def _gla_prompt_body(x_ref, g_ref, wq_ref, wk_ref, wv_ref, wr_ref, wz_ref, wg2_ref, bg_ref, s0_ref,
                     go_ref, wo_ref, *rest, layer, first):
    out_ref, so_ref, state, scores = rest[-4:]
    c = pl.program_id(1)
    seqs, chunk, _ = x_ref.shape

    @pl.when(c == 0)
    def _():
        state[...] = s0_ref[...]

    tri = (lax.broadcasted_iota(jnp.int32, (chunk, chunk), 1)
           <= lax.broadcasted_iota(jnp.int32, (chunk, chunk), 0)).astype(bf16)
    ones = jnp.ones((chunk, LANES), bf16)
    rows = slice(0, chunk)

    def projection(s):
        x = x_ref[s]
        h = _rms(x, g_ref[...]).astype(bf16)
        got = {"x": x}
        parts = {"q": [], "k": [], "v": [], "r": []}

        def gate():
            z = _dot(h, wz_ref[...])
            pre = _dot(z.astype(bf16), wg2_ref[...]) + bg_ref[...]
            got["la"] = (jnp.minimum(pre, 0.0) - jnp.log1p(jnp.exp(-jnp.abs(pre)))) * (1.0 / GLA_TAU)

        def column_tile(name, w_ref, c0, width, scale):
            def piece():
                out = _dot(h, w_ref[:, c0:c0 + MXU_COLS])
                parts[name].append(out * scale if scale is not None else out)
                if c0 + MXU_COLS == width:
                    got[name] = jnp.concatenate(parts[name], axis=1)
            return piece

        pieces = [gate]
        for name, w_ref, width, scale in (("q", wq_ref, GLA_DKT, GLA_DK ** -0.5), ("k", wk_ref, GLA_DKT, None),
                                          ("v", wv_ref, GLA_DVT, None), ("r", wr_ref, GLA_DVT, None)):
            pieces += [column_tile(name, w_ref, c0, width, scale) for c0 in range(0, width, MXU_COLS)]
        return got, pieces

    cur, pieces = projection(0)
    for piece in pieces:
        piece()
    for s in range(seqs):
        nxt, pieces = projection(s + 1) if s + 1 < seqs else (None, [])
        la_hi, la_lo = _split_bf16(cur["la"])
        gated = _gla_chunk(rows, cur["q"], cur["k"], cur["v"], cur["r"], la_hi, la_lo, tri, ones,
                           state.at[s], scores.at[s], go_ref, pieces)
        out_ref[s] = cur["x"] + _dot(gated, wo_ref[...])
        cur = nxt

    @pl.when(c == pl.num_programs(1) - 1)
    def _():
        _own_layer(so_ref, layer, first)[...] = state[...]


def _gla_chunk(rows, q, k, v, r, la_hi, la_lo, tri, ones, state, scores, go_ref, between=()):
    chunk = tri.shape[0]
    blocks = chunk // GLA_SUB
    slots = GLA_HEADS * blocks
    emit_at = {}
    for n, piece in enumerate(between):
        emit_at.setdefault(n * slots // len(between), []).append(piece)
    la_hi, la_lo = la_hi[rows], la_lo[rows]
    b_all = _dot(tri, la_hi) + _dot(tri, la_lo)
    gated = []
    for h in range(GLA_HEADS):
        ks = slice(h * GLA_DK, (h + 1) * GLA_DK)
        vs = slice(h * GLA_DV, (h + 1) * GLA_DV)
        b = b_all[:, ks]
        qh = q[rows, ks]
        kh = k[rows, ks]
        vb = v[rows, vs].astype(bf16)
        b_last = b[chunk - 1:chunk, :]
        q_dec = (qh * jnp.exp(b)).astype(bf16)
        k_end = (kh * jnp.exp(b_last - b)).astype(bf16)
        s_old = state[h]
        o_inter = _dot(q_dec, s_old.astype(bf16))

        for i in range(chunk // GLA_SUB):
            r0 = i * GLA_SUB
            nk = r0 + GLA_SUB
            npad = LANES * (-(-nk // LANES))
            bq = b[r0:nk]
            bk = b[0:nk]
            if i == 0:
                q_exp, k_exp = bq, -bk
            else:
                ref = b[r0 - 1:r0, :]
                q_exp, k_exp = bq - ref, ref - bk
            qi = (qh[r0:nk] * jnp.exp(q_exp)).astype(bf16)
            ki = (kh[0:nk] * jnp.exp(k_exp)).astype(bf16)
            if nk < npad:
                ki = jnp.concatenate([ki, jnp.zeros((npad - nk, GLA_DK), bf16)], axis=0)
            s = _dot_t_rhs(qi, ki)
            causal = (lax.broadcasted_iota(jnp.int32, (GLA_SUB, npad), 1)
                      <= lax.broadcasted_iota(jnp.int32, (GLA_SUB, npad), 0) + r0)
            scores[h, r0:nk, 0:npad] = jnp.where(causal, s, 0.0).astype(bf16)
            if npad < chunk:
                scores[h, r0:nk, npad:chunk] = jnp.zeros((GLA_SUB, chunk - npad), bf16)
            for piece in emit_at.get(h * blocks + i, ()):
                piece()

        o = o_inter + _dot(scores[h], vb)
        gated.append(_gla_gate(o, r[rows, vs], go_ref))
        state[h] = s_old * _decay_columns(la_hi[:, ks], la_lo[:, ks], ones) + _dot_t_lhs(k_end, vb)
    return jnp.concatenate(gated, axis=1)


def _gla_prompt(x, states, w, layer, j, prev, batch, seq):
    chunk, seqs = GLA_CHUNK, GLA_SEQS
    n_layers = states.shape[0]
    first = prev is None
    extra_specs, extra_args, aliases = _stacked(prev, 12, 1)
    seq_tile = pl.BlockSpec((seqs, chunk, D_MODEL), lambda i, c: (i, c, 0))
    state_shape = (seqs, GLA_HEADS, GLA_DK, GLA_DV)
    out, new_states = pl.pallas_call(
        functools.partial(_gla_prompt_body, layer=j, first=first),
        grid=(batch // seqs, seq // chunk),
        in_specs=[seq_tile, _vec(layer),
                  _mat(j, D_MODEL, GLA_DKT, 0), _mat(j, D_MODEL, GLA_DKT, 1),
                  _mat(j, D_MODEL, GLA_DVT, 1), _mat(j, D_MODEL, GLA_DVT, 2),
                  _mat(j, D_MODEL, LANES), _mat(j, LANES, GLA_DKT), _vec(j, GLA_DKT),
                  pl.BlockSpec((None,) + state_shape, lambda i, c: (j, i, 0, 0, 0)),
                  _vec(j, GLA_DV), _mat(j, GLA_DVT, D_MODEL)] + extra_specs,
        out_specs=[seq_tile, _layer_block(first, n_layers, j, state_shape, lambda i, c: (i, 0, 0, 0))],
        out_shape=[jax.ShapeDtypeStruct((batch, seq, D_MODEL), f32),
                   jax.ShapeDtypeStruct((n_layers, batch, GLA_HEADS, GLA_DK, GLA_DV), f32)],
        scratch_shapes=[pltpu.VMEM(state_shape, f32), pltpu.VMEM((seqs, GLA_HEADS, chunk, chunk), bf16)],
        input_output_aliases=aliases,
        compiler_params=_params("arbitrary", "arbitrary"),
        name="gla_prompt",
    )(x.reshape(batch, seq, D_MODEL), w["norm_mix"], w["gla_w_in"], w["gla_w_in"], w["gla_w_in"], w["gla_w_in"],
      w["gla_w_z"], w["gla_w_gate2"], w["gla_b_gate"], states, w["gla_onorm_g"], w["gla_w_out"], *extra_args)
    return out.reshape(batch * seq, D_MODEL), new_states


def _gla_b_sample_body(q_ref, k_ref, v_ref, la_ref, s_hbm, *rest, layer, first):
    o_ref, so_ref, ring, sems = rest[-4:]
    so = _own_layer(so_ref, layer, first)
    steps, bb, _ = q_ref.shape

    step, n_steps = pl.program_id(0), pl.num_programs(0)

    def fetch(t, slot):
        return pltpu.make_async_copy(s_hbm.at[layer, pl.ds(t * bb, bb)], ring.at[slot], sems.at[slot])

    @pl.when(step == 0)
    def _():
        for t in range(GLA_RING - 1):
            @pl.when(t < n_steps)
            def _():
                fetch(t, t).start()

    ahead = step + (GLA_RING - 1)

    @pl.when(ahead < n_steps)
    def _():
        fetch(ahead, ahead % GLA_RING).start()

    slot = step % GLA_RING
    fetch(step, slot).wait()
    s0_ref = ring.at[slot]
    causal = (lax.broadcasted_iota(jnp.int32, (steps, steps), 1)
              <= lax.broadcasted_iota(jnp.int32, (steps, steps), 0))
    ones = jnp.ones((steps, LANES), bf16)

    for i in range(bb):
        la = la_ref[:, i, :]
        q, k, v = q_ref[:, i, :], k_ref[:, i, :], v_ref[:, i, :]
        cum = [la[0:1]]
        for t in range(1, steps):
            cum.append(cum[-1] + la[t:t + 1])
        b_all = jnp.concatenate(cum, axis=0)
        la_hi, la_lo = _split_bf16(la)
        heads = []
        for h in range(GLA_HEADS):
            ks = slice(h * GLA_DK, (h + 1) * GLA_DK)
            vs = slice(h * GLA_DV, (h + 1) * GLA_DV)
            b = b_all[:, ks]
            qh = q[:, ks]
            kh = k[:, ks]
            vb = v[:, vs].astype(bf16)
            b_last = b[steps - 1:steps, :]
            q_dec = (qh * jnp.exp(b)).astype(bf16)
            k_inv = (kh * jnp.exp(-b)).astype(bf16)
            k_end = (kh * jnp.exp(b_last - b)).astype(bf16)
            s_old = s0_ref[i, h]
            sc = jnp.where(causal, _dot_t_rhs(q_dec, k_inv), 0.0).astype(bf16)
            heads.append(_dot(sc, vb) + _dot(q_dec, s_old.astype(bf16)))
            so[i, h] = s_old * _decay_columns(la_hi[:, ks], la_lo[:, ks], ones) + _dot_t_lhs(k_end, vb)
        o_ref[:, i, :] = jnp.concatenate(heads, axis=1)


def _gla_b_sample(q, k, v, la, states, j, prev):
    steps, batch, _ = q.shape
    bb = GLA_SAMPLE_BATCH
    n_layers = states.shape[0]
    first = prev is None
    extra_specs, extra_args, aliases = _stacked(prev, 5, 1)
    step_block = lambda n: pl.BlockSpec((steps, bb, n), lambda i: (0, i, 0))
    state_shape = (bb, GLA_HEADS, GLA_DK, GLA_DV)
    return pl.pallas_call(
        functools.partial(_gla_b_sample_body, layer=j, first=first),
        grid=(batch // bb,),
        in_specs=[step_block(GLA_DKT), step_block(GLA_DKT), step_block(GLA_DVT), step_block(GLA_DKT),
                  pl.BlockSpec(memory_space=pl.ANY)] + extra_specs,
        out_specs=[step_block(GLA_DVT),
                   _layer_block(first, n_layers, j, state_shape, lambda i: (i, 0, 0, 0))],
        out_shape=[jax.ShapeDtypeStruct((steps, batch, GLA_DVT), f32),
                   jax.ShapeDtypeStruct((n_layers, batch, GLA_HEADS, GLA_DK, GLA_DV), f32)],
        scratch_shapes=[pltpu.VMEM((GLA_RING,) + state_shape, f32), pltpu.SemaphoreType.DMA((GLA_RING,))],
        input_output_aliases=aliases,
        compiler_params=_params("arbitrary"),
        name="gla_core_sample",
    )(q, k, v, la, states, *extra_args)


def _gla_gate(o, r, go_ref):
    return (_rms(o, go_ref[...]) * _silu(r)).astype(bf16)


def _gla_c_body(o_ref, r_ref, x_ref, go_ref, wo_ref, out_ref):
    parts = []
    for h in range(GLA_HEADS):
        vs = slice(h * GLA_DV, (h + 1) * GLA_DV)
        parts.append(_gla_gate(o_ref[:, vs], r_ref[:, vs], go_ref))
    out_ref[...] = x_ref[...] + _dot(jnp.concatenate(parts, axis=1), wo_ref[...])


def _gla_c(o, r, x, w, j):
    rows = x.shape[0]
    tile = min(ROW_TILE, rows)
    return pl.pallas_call(
        _gla_c_body,
        grid=(rows // tile,),
        in_specs=[_rows(tile, GLA_DVT), _rows(tile, GLA_DVT), _rows(tile, D_MODEL), _vec(j, GLA_DV),
                  _mat(j, GLA_DVT, D_MODEL)],
        out_specs=_rows(tile, D_MODEL),
        out_shape=jax.ShapeDtypeStruct((rows, D_MODEL), f32),
        compiler_params=_params("arbitrary"),
        name="gla_out",
    )(o, r, x, w["gla_onorm_g"], w["gla_w_out"])


def _mix_sample(x, conv_state, gla_state, w, i, new_conv, new_gla, batch, steps):
    j = i // 2
    if i % 2 == 0:
        u = _conv_a(x, w, i, j)
        c, new_conv = _conv_b_sample(u.reshape(steps, batch, D_MODEL), conv_state, w, j, new_conv)
        x = _conv_c(c.reshape(steps * batch, D_MODEL), x, w, j)
    else:
        q, k, v, r, la = _gla_a(x, w, i, j)
        by_step = lambda a: a.reshape(steps, batch, a.shape[-1])
        o, new_gla = _gla_b_sample(by_step(q), by_step(k), by_step(v), by_step(la), gla_state, j, new_gla)
        x = _gla_c(o.reshape(steps * batch, GLA_DVT), r, x, w, j)
    return x, new_conv, new_gla


def _mix_prompt(x, conv_state, gla_state, w, i, new_conv, new_gla, batch, seq):
    j = i // 2
    if i % 2 == 0:
        x, new_conv = _conv_prompt(x, conv_state, w, i, j, new_conv, batch, seq)
    else:
        x, new_gla = _gla_prompt(x, gla_state, w, i, j, new_gla, batch, seq)
    return x, new_conv, new_gla


def kernel(x_prompt, x_sample, state_conv, state_gla, norm_ffn_pre, norm_mix, norm_ffn_post, norm_final,
           ffn_w_gate, ffn_w_up, ffn_w_down, conv_w_pw1, conv_b_pw1, conv_w_dw, conv_b_dw, conv_ln_g, conv_ln_b,
           conv_w_pw2, conv_b_pw2, gla_w_in, gla_w_gate2, gla_b_gate, gla_onorm_g, gla_w_out):
    row = lambda a: a[..., None, :]
    z_start = 2 * GLA_DKT + 2 * GLA_DVT
    w_z = jnp.pad(gla_w_in[:, :, z_start:], ((0, 0), (0, 0), (0, LANES - GLA_GATE_RANK)))
    w_gate2 = jnp.pad(gla_w_gate2, ((0, 0), (0, LANES - GLA_GATE_RANK), (0, 0)))
    w = dict(
        norm_ffn_pre=row(norm_ffn_pre), norm_mix=row(norm_mix), norm_ffn_post=row(norm_ffn_post),
        norm_final=norm_final[None, :],
        ffn_w_gate=ffn_w_gate, ffn_w_up=ffn_w_up, ffn_w_down=ffn_w_down,
        conv_w_pw1=conv_w_pw1.astype(bf16), conv_b_pw1=row(conv_b_pw1),
        conv_w_dw=conv_w_dw, conv_b_dw=row(conv_b_dw), conv_ln_g=row(conv_ln_g), conv_ln_b=row(conv_ln_b),
        conv_w_pw2=conv_w_pw2.astype(bf16), conv_b_pw2=row(conv_b_pw2),
        gla_w_in=gla_w_in.astype(bf16), gla_w_z=w_z.astype(bf16), gla_w_gate2=w_gate2.astype(bf16),
        gla_b_gate=row(gla_b_gate), gla_onorm_g=row(gla_onorm_g), gla_w_out=gla_w_out.astype(bf16),
    )
    batch, seq, _ = x_prompt.shape
    dec_batch, steps, _ = x_sample.shape
    conv0 = jnp.zeros((state_conv.shape[0], batch) + state_conv.shape[2:], x_prompt.dtype)
    gla0 = jnp.zeros((state_gla.shape[0], batch) + state_gla.shape[2:], x_prompt.dtype)
    conv_in_s = state_conv.transpose(0, 2, 1, 3)
    xp = x_prompt.reshape(batch * seq, D_MODEL)
    xs = x_sample.transpose(1, 0, 2).reshape(steps * dec_batch, D_MODEL)

    wb = tuple(a[0, 0].astype(bf16) for a in (ffn_w_gate, ffn_w_up, ffn_w_down))
    conv_p = gla_p = conv_s = gla_s = None
    for i in range(DEPTH):
        xp, xs, wb = _ffn(xp, xs, wb, w, "norm_ffn_pre", i, (i, 1))
        xp, conv_p, gla_p = _mix_prompt(xp, conv0, gla0, w, i, conv_p, gla_p, batch, seq)
        xs, conv_s, gla_s = _mix_sample(xs, conv_in_s, state_gla, w, i, conv_s, gla_s, dec_batch, steps)
        last = i == DEPTH - 1
        xp, xs, wb = _ffn(xp, xs, wb, w, "norm_ffn_post", i, None if last else (i + 1, 0), final_norm=last)

    y_p = xp.reshape(batch, seq, D_MODEL)
    y_s = xs.reshape(steps, dec_batch, D_MODEL).transpose(1, 0, 2)
    return (y_p, y_s, conv_p, gla_p, conv_s.transpose(0, 2, 1, 3), gla_s)
```
